```python
import jax, jax.numpy as jnp
from jax import lax
import numpy as np

D_MODEL = 2048
BATCH = 4
SEQ = 4096
DEPTH = 2

N_A_LAYERS = DEPTH // 2
N_B_LAYERS = DEPTH - N_A_LAYERS

SSD_EXPAND = 2
SSD_D_INNER = SSD_EXPAND * D_MODEL
SSD_HEAD_DIM = 64
SSD_N_HEADS = SSD_D_INNER // SSD_HEAD_DIM
SSD_N_GROUPS = 8
SSD_HEADS_PER_GROUP = SSD_N_HEADS // SSD_N_GROUPS
SSD_D_STATE = 128
SSD_CONV_W = 4
SSD_CHUNK = 256
SSD_BC_DIM = SSD_N_GROUPS * SSD_D_STATE
SSD_CONV_DIM = SSD_D_INNER + 2 * SSD_BC_DIM
SSD_IN_DIM = SSD_D_INNER + SSD_CONV_DIM + SSD_N_HEADS

DIL_PATTERNS = ((128, 1), (512, 4), (2048, 16))
DIL_N_GROUPS = len(DIL_PATTERNS)
DIL_HEADS = 8
DIL_HEAD_DIM = 128
DIL_Q_WIDTH = DIL_N_GROUPS * DIL_HEADS * DIL_HEAD_DIM
DIL_OUT_WIDTH = DIL_HEADS * DIL_HEAD_DIM
DIL_IN_DIM = DIL_Q_WIDTH + DIL_OUT_WIDTH
DIL_KV_DIM = 2 * DIL_Q_WIDTH
DIL_BLOCK = 128

DEEPNORM_ALPHA = (2 * DEPTH) ** 0.25
DEEPNORM_BETA = (8 * DEPTH) ** -0.25
LN_EPS = 1e-5
RMS_EPS = 1e-5

kernel_name = "hybrid_yoco_ssd_dilated_alibi_deepnorm"


def _layer_norm(x, g, b):
    xf = x.astype(jnp.float32)
    mu = jnp.mean(xf, -1, keepdims=True)
    var = jnp.mean(jnp.square(xf - mu), -1, keepdims=True)
    return ((xf - mu) * lax.rsqrt(var + LN_EPS)).astype(x.dtype) * g + b


def _adaln(c, w, b):
    mod = jax.nn.silu(c) @ w + b
    shift, scale, gate = jnp.split(mod, 3, axis=-1)
    return shift[:, None, :], scale[:, None, :], gate[:, None, :]


def _causal_depthwise_conv(x, w, b):
    y = lax.conv_general_dilated(
        x, w[:, None, :], window_strides=(1,), padding=[(SSD_CONV_W - 1, 0)],
        dimension_numbers=("NWC", "WIO", "NWC"), feature_group_count=x.shape[-1])
    return y + b


def _ssd_chunked(xdt, dtA, Bm, Cm):
    f32 = jnp.float32
    bsz, L = xdt.shape[:2]
    G, K, P, N = SSD_N_GROUPS, SSD_HEADS_PER_GROUP, SSD_HEAD_DIM, SSD_D_STATE
    Lp = -(-L // SSD_CHUNK) * SSD_CHUNK
    nc = Lp // SSD_CHUNK

    def chunks(a):
        a = jnp.pad(a, [(0, 0), (0, Lp - L)] + [(0, 0)] * (a.ndim - 2))
        a = a.reshape((bsz, nc, SSD_CHUNK) + a.shape[2:])
        return jnp.moveaxis(a, 1, 0)

    xs = chunks(xdt.reshape(bsz, L, G, K, P).astype(f32))
    As = chunks(dtA.reshape(bsz, L, G, K).astype(f32))
    Bs = chunks(Bm.astype(f32))
    Cs = chunks(Cm.astype(f32))
    causal = jnp.tril(jnp.ones((SSD_CHUNK, SSD_CHUNK), bool))[None, :, :, None, None]

    def step(state, inp):
        xc, ac, bc, cc = inp
        acum = jnp.cumsum(ac, axis=1)
        seg = acum[:, :, None] - acum[:, None, :]
        decay = jnp.exp(jnp.where(causal, seg, -jnp.inf))
        cb = jnp.einsum("blgn,bsgn->blsg", cc, bc)
        y_diag = jnp.einsum("blsgk,bsgkp->blgkp", cb[..., None] * decay, xc)
        y_off = jnp.einsum("blgn,bgkpn->blgkp", cc, state) * jnp.exp(acum)[..., None]
        tail = jnp.exp(acum[:, -1:] - acum)
        new_state = (state * jnp.exp(acum[:, -1])[..., None, None]
                     + jnp.einsum("bsgn,bsgkp->bgkpn", bc, xc * tail[..., None]))
        return new_state, y_diag + y_off

    state0 = jnp.zeros((bsz, G, K, P, N), f32)
    _, ys = lax.scan(step, state0, (xs, As, Bs, Cs))
    ys = jnp.moveaxis(ys, 0, 1).reshape(bsz, Lp, SSD_N_HEADS * P)
    return ys[:, :L]


def _ssd_mixer(h, in_w, conv_w, conv_b, dt_bias, A_log, D_skip, norm_g, out_w):
    bsz, L, _ = h.shape
    proj = h @ in_w
    z, xBC, dt = jnp.split(proj, [SSD_D_INNER, SSD_D_INNER + SSD_CONV_DIM], axis=-1)
    xBC = jax.nn.silu(_causal_depthwise_conv(xBC, conv_w, conv_b))
    xs, Bm, Cm = jnp.split(xBC, [SSD_D_INNER, SSD_D_INNER + SSD_BC_DIM], axis=-1)
    xs = xs.reshape(bsz, L, SSD_N_HEADS, SSD_HEAD_DIM)
    Bm = Bm.reshape(bsz, L, SSD_N_GROUPS, SSD_D_STATE)
    Cm = Cm.reshape(bsz, L, SSD_N_GROUPS, SSD_D_STATE)
    dt = jax.nn.softplus((dt + dt_bias).astype(jnp.float32))
    A = -jnp.exp(A_log.astype(jnp.float32))
    y = _ssd_chunked(xs * dt[..., None], dt * A, Bm, Cm)
    y = y + (xs * D_skip[:, None]).reshape(bsz, L, SSD_D_INNER)
    y = y * jax.nn.silu(z.astype(jnp.float32))
    yg = y.reshape(bsz, L, SSD_N_GROUPS, -1)
    yg = yg * lax.rsqrt(jnp.mean(jnp.square(yg), -1, keepdims=True) + RMS_EPS)
    y = yg.reshape(bsz, L, SSD_D_INNER).astype(h.dtype) * norm_g
    return y @ out_w


def _alibi_slopes():
    n = DIL_N_GROUPS * DIL_HEADS
    s = 2.0 ** (-8.0 * np.arange(1, n + 1) / n)
    return jnp.asarray(s.reshape(DIL_N_GROUPS, DIL_HEADS), dtype=jnp.float32)


def _dilated_window_attention(q, k, v, window, dilation, slopes):
    f32 = jnp.float32
    bsz, L, H, E = q.shape
    span = window // dilation
    M = -(-L // (dilation * DIL_BLOCK)) * DIL_BLOCK
    nb = M // DIL_BLOCK
    pad = M * dilation - L

    def to_blocks(a):
        a = jnp.pad(a, [(0, 0), (0, pad), (0, 0), (0, 0)])
        a = a.reshape(bsz, M, dilation, H, E).transpose(0, 2, 1, 3, 4)
        return a.reshape(bsz, dilation, nb, DIL_BLOCK, H, E)

    def with_prev(a):
        prev = jnp.pad(a[:, :, :-1], [(0, 0), (0, 0), (1, 0), (0, 0), (0, 0), (0, 0)])
        return jnp.concatenate([prev, a], axis=3)

    qb = to_blocks(q)
    kb = with_prev(to_blocks(k))
    vb = with_prev(to_blocks(v))
    s = jnp.einsum("brnqhe,brnkhe->brnhqk", qb, kb,
                   preferred_element_type=f32) * (E ** -0.5)
    qi = jnp.arange(DIL_BLOCK)[:, None]
    kj = jnp.arange(2 * DIL_BLOCK)[None, :]
    delta = qi + DIL_BLOCK - kj
    valid = (delta >= 0) & (delta <= span)
    first = (jnp.arange(nb) == 0)[:, None, None]
    valid = valid[None] & ~(first & (kj < DIL_BLOCK)[None])
    alibi = -slopes[:, None, None] * (delta * dilation).astype(f32)[None]
    s = jnp.where(valid[None, None, :, None], s + alibi[None, None, None], -jnp.inf)
    m = jnp.max(s, -1, keepdims=True)
    p = jnp.exp(s - m)
    den = jnp.sum(p, -1)
    o = jnp.einsum("brnhqk,brnkhe->brnqhe", p, vb.astype(f32))
    o = o / jnp.moveaxis(den, 3, 4)[..., None]
    lse = jnp.moveaxis(m[..., 0] + jnp.log(den), 3, 4)

    def from_blocks(a):
        a = a.reshape((bsz, dilation, M) + a.shape[4:])
        a = jnp.moveaxis(a, 1, 2).reshape((bsz, M * dilation) + a.shape[3:])
        return a[:, :L]

    return from_blocks(o), from_blocks(lse)


def _shared_kv(x, kv_w):
    bsz, L, _ = x.shape
    k, v = jnp.split(x @ kv_w, 2, axis=-1)
    shp = (bsz, L, DIL_N_GROUPS, DIL_HEADS, DIL_HEAD_DIM)
    return k.reshape(shp), v.reshape(shp)


def _dilated_mixer(h, k_sh, v_sh, in_w, out_w):
    bsz, L, _ = h.shape
    q, z = jnp.split(h @ in_w, [DIL_Q_WIDTH], axis=-1)
    q = q.reshape(bsz, L, DIL_N_GROUPS, DIL_HEADS, DIL_HEAD_DIM)
    slopes = _alibi_slopes()
    outs, lses = [], []
    for g, (window, dilation) in enumerate(DIL_PATTERNS):
        o, lse = _dilated_window_attention(q[:, :, g], k_sh[:, :, g], v_sh[:, :, g],
                                           window, dilation, slopes[g])
        outs.append(o)
        lses.append(lse)
    o = jnp.stack(outs, 2)
    wts = jax.nn.softmax(jnp.stack(lses, 2), axis=2)
    o = jnp.einsum("blghe,blgh->blhe", o, wts).reshape(bsz, L, DIL_OUT_WIDTH)
    o = o.astype(h.dtype) * jax.nn.silu(z)
    return o @ out_w


def setup_inputs(seed: int = 0) -> dict:
    key = jax.random.key(seed)
    ks = jax.random.split(key, 20)
    f32 = jnp.float32
    nA, nB, D = N_A_LAYERS, N_B_LAYERS, D_MODEL
    nrm = lambda k, shp, sc: jax.random.normal(k, shp, f32) * sc
    dt0 = jnp.exp(jax.random.uniform(ks[8], (nA, SSD_N_HEADS), f32,
                                     np.log(1e-3), np.log(1e-1)))
    return {
        "x": nrm(ks[0], (BATCH, SEQ, D), 1.0),
        "c": nrm(ks[1], (BATCH, D), 1.0),
        "ada_w": nrm(ks[2], (DEPTH, D, 3 * D), 0.1 * D ** -0.5),
        "ada_b": nrm(ks[3], (DEPTH, 3 * D), 0.01),
        "ln_g": 1.0 + nrm(ks[4], (DEPTH, D), 0.01),
        "ln_b": nrm(ks[5], (DEPTH, D), 0.01),
        "a_in_w": nrm(ks[6], (nA, D, SSD_IN_DIM), D ** -0.5),
        "a_conv_w": nrm(ks[7], (nA, SSD_CONV_W, SSD_CONV_DIM), SSD_CONV_W ** -0.5),
        "a_conv_b": nrm(ks[9], (nA, SSD_CONV_DIM), 0.01),
        "a_dt_bias": dt0 + jnp.log(-jnp.expm1(-dt0)),
        "a_A_log": jnp.log(jax.random.uniform(ks[10], (nA, SSD_N_HEADS), f32, 1.0, 16.0)),
        "a_D": 1.0 + nrm(ks[11], (nA, SSD_N_HEADS), 0.01),
        "a_norm_g": 1.0 + nrm(ks[12], (nA, SSD_D_INNER), 0.01),
        "a_out_w": nrm(ks[13], (nA, SSD_D_INNER, D), DEEPNORM_BETA * SSD_D_INNER ** -0.5),
        "kv_w": nrm(ks[14], (D, DIL_KV_DIM), D ** -0.5),
        "b_in_w": nrm(ks[15], (nB, D, DIL_IN_DIM), D ** -0.5),
        "b_out_w": nrm(ks[16], (nB, DIL_OUT_WIDTH, D), DEEPNORM_BETA * DIL_OUT_WIDTH ** -0.5),
    }


def reference(x, c, ada_w, ada_b, ln_g, ln_b, a_in_w, a_conv_w, a_conv_b, a_dt_bias,
              a_A_log, a_D, a_norm_g, a_out_w, kv_w, b_in_w, b_out_w):
    k_sh, v_sh = None, None
    for layer in range(DEPTH):
        shift, scale, gate = _adaln(c, ada_w[layer], ada_b[layer])
        h = x * (1.0 + scale) + shift
        if layer < N_A_LAYERS:
            i = layer
            y = _ssd_mixer(h, a_in_w[i], a_conv_w[i], a_conv_b[i], a_dt_bias[i],
                           a_A_log[i], a_D[i], a_norm_g[i], a_out_w[i])
        else:
            i = layer - N_A_LAYERS
            y = _dilated_mixer(h, k_sh, v_sh, b_in_w[i], b_out_w[i])
        x = _layer_norm(DEEPNORM_ALPHA * x + (1.0 + gate) * y, ln_g[layer], ln_b[layer])
        if layer == N_A_LAYERS - 1:
            k_sh, v_sh = _shared_kv(x, kv_w)
    return x
```

```python
import functools

import numpy as np
import jax
import jax.numpy as jnp
from jax import lax
from jax.experimental import pallas as pl
from jax.experimental.pallas import tpu as pltpu

F32 = jnp.float32
BF16 = jnp.bfloat16

D_MODEL = 2048
DEPTH = 2
LANES = 128
SUBLANES = 8

SSD_D_INNER = 2 * D_MODEL
SSD_HEAD_DIM = 64
SSD_N_HEADS = SSD_D_INNER // SSD_HEAD_DIM
SSD_N_GROUPS = 8
SSD_HEADS_PER_GROUP = SSD_N_HEADS // SSD_N_GROUPS
SSD_D_STATE = 128
SSD_CONV_W = 4
SSD_CHUNK = 256
SSD_BC_DIM = SSD_N_GROUPS * SSD_D_STATE
SSD_CONV_DIM = SSD_D_INNER + 2 * SSD_BC_DIM
SSD_GROUP_W = SSD_HEADS_PER_GROUP * SSD_HEAD_DIM

DIL_PATTERNS = ((128, 1), (512, 4), (2048, 16))
DIL_N_GROUPS = len(DIL_PATTERNS)
DIL_HEADS = 8
DIL_HEAD_DIM = 128
DIL_Q_WIDTH = DIL_N_GROUPS * DIL_HEADS * DIL_HEAD_DIM
DIL_OUT_WIDTH = DIL_HEADS * DIL_HEAD_DIM
DIL_BLOCK = 128

DEEPNORM_ALPHA = (2 * DEPTH) ** 0.25
LN_EPS = 1e-5
RMS_EPS = 1e-5

VMEM_LIMIT = 56 * 1024 * 1024


def _params(sem):
    return pltpu.CompilerParams(dimension_semantics=sem, vmem_limit_bytes=VMEM_LIMIT)


def _silu(v):
    return v * (1.0 / (1.0 + jnp.exp(-v)))


def _split3(v):
    v1 = v.astype(BF16)
    r1 = v - v1.astype(F32)
    v2 = r1.astype(BF16)
    r2 = r1 - v2.astype(F32)
    v3 = r2.astype(BF16)
    return jnp.concatenate([v1, v2, v3], axis=1)


def _layer_norm(u, g, b):
    mu = jnp.mean(u, axis=-1, keepdims=True)
    d = u - mu
    var = jnp.mean(d * d, axis=-1, keepdims=True)
    return d * lax.rsqrt(var + LN_EPS) * g + b


def _adaln_kernel(c_ref, w_ref, b_ref, o_ref):
    s = _silu(c_ref[...]).astype(BF16)
    o_ref[...] = jnp.dot(s, w_ref[...].astype(BF16), preferred_element_type=F32) + b_ref[...]


def _adaln(c_pad, ada_w, ada_b):
    rows = c_pad.shape[0]
    n = 3 * D_MODEL
    tn = 768
    return pl.pallas_call(
        _adaln_kernel,
        grid=(DEPTH, n // tn),
        in_specs=[
            pl.BlockSpec((rows, D_MODEL), lambda l, j: (0, 0)),
            pl.BlockSpec((None, D_MODEL, tn), lambda l, j: (l, 0, j)),
            pl.BlockSpec((None, 1, tn), lambda l, j: (l, 0, j)),
        ],
        out_specs=pl.BlockSpec((None, rows, tn), lambda l, j: (l, 0, j)),
        out_shape=jax.ShapeDtypeStruct((DEPTH, rows, n), F32),
        compiler_params=_params(("parallel", "parallel")),
        name="adaln",
    )(c_pad, ada_w, ada_b.reshape(DEPTH, 1, n))


def _mm_kernel(*refs, modulate, with_dt):
    if with_dt:
        x_ref, scale_ref, shift_ref, w_ref, wdt_ref, o_ref, dt_ref, h_ref = refs
    else:
        x_ref, scale_ref, shift_ref, w_ref, o_ref, h_ref = refs

    @pl.when(pl.program_id(1) == 0)
    def _():
        x = x_ref[...]
        if modulate:
            x = x * (1.0 + scale_ref[...]) + shift_ref[...]
        h = x.astype(BF16)
        h_ref[...] = h
        if with_dt:
            dt_ref[...] = jnp.dot(h, wdt_ref[...], preferred_element_type=F32)

    o_ref[...] = jnp.dot(h_ref[...], w_ref[...], preferred_element_type=F32).astype(o_ref.dtype)


def _mod_matmul(x2, scale, shift, w, out_dtype, *, seq, modulate, tm, tn, wdt=None, name):
    t, d = x2.shape
    n = w.shape[1]
    per_batch = seq // tm
    with_dt = wdt is not None
    in_specs = [
        pl.BlockSpec((tm, d), lambda i, j: (i, 0)),
        pl.BlockSpec((None, 1, d), lambda i, j: (i // per_batch, 0, 0)),
        pl.BlockSpec((None, 1, d), lambda i, j: (i // per_batch, 0, 0)),
        pl.BlockSpec((d, tn), lambda i, j: (0, j)),
    ]
    args = [x2, scale, shift, w]
    out_specs = [pl.BlockSpec((tm, tn), lambda i, j: (i, j))]
    out_shape = [jax.ShapeDtypeStruct((t, n), out_dtype)]
    if with_dt:
        in_specs.append(pl.BlockSpec((d, LANES), lambda i, j: (0, 0)))
        args.append(wdt)
        out_specs.append(pl.BlockSpec((tm, LANES), lambda i, j: (i, 0)))
        out_shape.append(jax.ShapeDtypeStruct((t, LANES), F32))
    res = pl.pallas_call(
        functools.partial(_mm_kernel, modulate=modulate, with_dt=with_dt),
        grid=(t // tm, n // tn),
        in_specs=in_specs,
        out_specs=out_specs,
        out_shape=out_shape,
        scratch_shapes=[pltpu.VMEM((tm, d), BF16)],
        compiler_params=_params(("parallel", "arbitrary")),
        name=name,
    )(*args)
    return res if with_dt else res[0]


def _conv_silu(raw, carry, w, b):
    acc = raw * w[SSD_CONV_W - 1:SSD_CONV_W, :] + b
    row = lax.broadcasted_iota(jnp.int32, carry.shape, 0)
    for s in range(1, SSD_CONV_W):
        sh = pltpu.roll(raw, s, axis=0)
        prev = pltpu.roll(carry, s, axis=0)
        head = jnp.where(row < s, prev, sh[:SUBLANES])
        sh = jnp.concatenate([head, sh[SUBLANES:]], axis=0)
        acc = acc + sh * w[SSD_CONV_W - 1 - s:SSD_CONV_W - s, :]
    return _silu(acc)


def _ssd_kernel(z_ref, x_ref, b_ref, c_ref, dt_ref,
                cwx_ref, cwb_ref, cwc_ref, cbx_ref, cbb_ref, cbc_ref,
                dtb_ref, alog_ref, dexp_ref, ng_ref, tril_ref, e_ref, sel_ref,
                y_ref,
                state_ref, carx_ref, carb_ref, carc_ref, v3_ref, ac3_ref):
    c = pl.program_id(1)
    g = pl.program_id(2)
    L = SSD_CHUNK

    @pl.when(c == 0)
    def _():
        state_ref[g] = jnp.zeros(state_ref.shape[1:], F32)
        carx_ref[g] = jnp.zeros(carx_ref.shape[1:], F32)
        carb_ref[g] = jnp.zeros(carb_ref.shape[1:], F32)
        carc_ref[g] = jnp.zeros(carc_ref.shape[1:], F32)

    @pl.when(g == 0)
    def _():
        v = dt_ref[...] + dtb_ref[...]
        dt = jnp.maximum(v, 0.0) + jnp.log1p(jnp.exp(-jnp.abs(v)))
        a = dt * (-jnp.exp(alog_ref[...]))
        a3 = _split3(a)
        tril = tril_ref[...]
        acum = (jnp.dot(tril, a3[:, :LANES], preferred_element_type=F32)
                + jnp.dot(tril, a3[:, LANES:2 * LANES], preferred_element_type=F32)
                + jnp.dot(tril, a3[:, 2 * LANES:], preferred_element_type=F32))
        tail = jnp.exp(acum[L - 1:L, :] - acum)
        v3_ref[0:L, :] = _split3(dt)
        v3_ref[L:2 * L, :] = _split3(tail)
        v3_ref[2 * L:3 * L, :] = _split3(jnp.exp(acum))
        ac3_ref[...] = _split3(acum)

    ex = jnp.dot(v3_ref[...], e_ref[...], preferred_element_type=F32)
    dt_e = ex[0:L]
    tail_e = ex[L:2 * L]
    eac_e = ex[2 * L:3 * L]
    acg = jnp.dot(ac3_ref[...], sel_ref[...], preferred_element_type=F32)
    acg_t = acg.T

    x_raw = x_ref[...]
    b_raw = b_ref[...]
    c_raw = c_ref[...]
    xg = _conv_silu(x_raw, carx_ref[g], cwx_ref[...], cbx_ref[...])
    bg = _conv_silu(b_raw, carb_ref[g], cwb_ref[...], cbb_ref[...])
    cg = _conv_silu(c_raw, carc_ref[g], cwc_ref[...], cbc_ref[...])
    carx_ref[g] = x_raw[L - SUBLANES:, :]
    carb_ref[g] = b_raw[L - SUBLANES:, :]
    carc_ref[g] = c_raw[L - SUBLANES:, :]

    xdt = xg * dt_e
    cb16 = cg.astype(BF16)
    bb16 = bg.astype(BF16)
    cb = lax.dot_general(cb16, bb16, (((1,), (1,)), ((), ())), preferred_element_type=F32)

    state = state_ref[g]
    y_off = jnp.dot(cb16, state.astype(BF16), preferred_element_type=F32) * eac_e

    li = lax.broadcasted_iota(jnp.int32, (L, L), 0)
    si = lax.broadcasted_iota(jnp.int32, (L, L), 1)
    causal = li >= si
    lane = lax.broadcasted_iota(jnp.int32, (L, LANES), 1)
    heads_per_tile = LANES // SSD_HEAD_DIM
    pieces = []
    for pair in range(SSD_GROUP_W // LANES):
        xp = xdt[:, pair * LANES:(pair + 1) * LANES]
        acc = None
        for sub in range(heads_per_tile):
            k = pair * heads_per_tile + sub
            seg = acg[:, k:k + 1] - acg_t[k:k + 1, :]
            decay = jnp.exp(jnp.where(causal, seg, -jnp.inf))
            m = (cb * decay).astype(BF16)
            in_head = (lane >= sub * SSD_HEAD_DIM) & (lane < (sub + 1) * SSD_HEAD_DIM)
            xm = jnp.where(in_head, xp, 0.0).astype(BF16)
            part = jnp.dot(m, xm, preferred_element_type=F32)
            acc = part if acc is None else acc + part
        pieces.append(acc)
    y = jnp.concatenate(pieces, axis=1) + y_off + xg * dexp_ref[...]

    xtail = (xdt * tail_e).astype(BF16)
    upd = lax.dot_general(bb16, xtail, (((0,), (0,)), ((), ())), preferred_element_type=F32)
    state_ref[g] = state * eac_e[L - 1:L, :] + upd

    y = y * _silu(z_ref[...])
    y = y * lax.rsqrt(jnp.mean(y * y, axis=-1, keepdims=True) + RMS_EPS)
    y_ref[...] = (y * ng_ref[...]).astype(y_ref.dtype)


def _ssd_constants():
    tril = np.tril(np.ones((SSD_CHUNK, SSD_CHUNK), np.float32))
    head_of_col = np.arange(SSD_D_INNER) // SSD_HEAD_DIM
    e = (np.arange(LANES)[:, None] == head_of_col[None, :]).astype(np.float32)
    col = np.arange(SSD_N_GROUPS * LANES)
    src = np.where(col % LANES < SSD_HEADS_PER_GROUP,
                   (col // LANES) * SSD_HEADS_PER_GROUP + col % LANES, -1)
    sel = (np.arange(LANES)[:, None] == src[None, :]).astype(np.float32)
    e3 = np.concatenate([e, e, e], axis=0)
    sel3 = np.concatenate([sel, sel, sel], axis=0)
    return jnp.asarray(tril, BF16), jnp.asarray(e3, BF16), jnp.asarray(sel3, BF16)


def _ssd_scan(zx, dt, conv_w, conv_b, dt_bias, a_log, d_skip, norm_g, *, batch, seq):
    t = zx.shape[0]
    L = SSD_CHUNK
    nc = seq // L
    G = SSD_N_GROUPS
    gw = SSD_GROUP_W
    ns = SSD_D_STATE
    tril, e3, sel3 = _ssd_constants()
    pad = LANES - SSD_N_HEADS
    dtb = jnp.pad(dt_bias, (0, pad)).reshape(1, LANES)
    alog = jnp.pad(a_log, (0, pad)).reshape(1, LANES)
    dexp = jnp.repeat(d_skip, SSD_HEAD_DIM).reshape(1, SSD_D_INNER)
    ng = norm_g.reshape(1, SSD_D_INNER)
    cb2 = conv_b.reshape(1, SSD_CONV_DIM)

    x_off = SSD_D_INNER // gw
    b_off = (2 * SSD_D_INNER) // ns
    c_off = (2 * SSD_D_INNER + SSD_BC_DIM) // ns
    wb_off = SSD_D_INNER // ns
    wc_off = (SSD_D_INNER + SSD_BC_DIM) // ns

    def row(b, c, g):
        return b * nc + c

    in_specs = [
        pl.BlockSpec((L, gw), lambda b, c, g: (row(b, c, g), g)),
        pl.BlockSpec((L, gw), lambda b, c, g: (row(b, c, g), x_off + g)),
        pl.BlockSpec((L, ns), lambda b, c, g: (row(b, c, g), b_off + g)),
        pl.BlockSpec((L, ns), lambda b, c, g: (row(b, c, g), c_off + g)),
        pl.BlockSpec((L, LANES), lambda b, c, g: (row(b, c, g), 0)),
        pl.BlockSpec((SSD_CONV_W, gw), lambda b, c, g: (0, g)),
        pl.BlockSpec((SSD_CONV_W, ns), lambda b, c, g: (0, wb_off + g)),
        pl.BlockSpec((SSD_CONV_W, ns), lambda b, c, g: (0, wc_off + g)),
        pl.BlockSpec((1, gw), lambda b, c, g: (0, g)),
        pl.BlockSpec((1, ns), lambda b, c, g: (0, wb_off + g)),
        pl.BlockSpec((1, ns), lambda b, c, g: (0, wc_off + g)),
        pl.BlockSpec((1, LANES), lambda b, c, g: (0, 0)),
        pl.BlockSpec((1, LANES), lambda b, c, g: (0, 0)),
        pl.BlockSpec((1, gw), lambda b, c, g: (0, g)),
        pl.BlockSpec((1, gw), lambda b, c, g: (0, g)),
        pl.BlockSpec((L, L), lambda b, c, g: (0, 0)),
        pl.BlockSpec((3 * LANES, gw), lambda b, c, g: (0, g)),
        pl.BlockSpec((3 * LANES, LANES), lambda b, c, g: (0, g)),
    ]
    return pl.pallas_call(
        _ssd_kernel,
        grid=(batch, nc, G),
        in_specs=in_specs,
        out_specs=pl.BlockSpec((L, gw), lambda b, c, g: (row(b, c, g), g)),
        out_shape=jax.ShapeDtypeStruct((t, SSD_D_INNER), BF16),
        scratch_shapes=[
            pltpu.VMEM((G, ns, gw), F32),
            pltpu.VMEM((G, SUBLANES, gw), F32),
            pltpu.VMEM((G, SUBLANES, ns), F32),
            pltpu.VMEM((G, SUBLANES, ns), F32),
            pltpu.VMEM((3 * L, 3 * LANES), BF16),
            pltpu.VMEM((L, 3 * LANES), BF16),
        ],
        compiler_params=_params(("parallel", "arbitrary", "arbitrary")),
        name="ssd_scan",
    )(zx, zx, zx, zx, dt, conv_w, conv_w, conv_w, cb2, cb2, cb2, dtb, alog, dexp, ng, tril, e3, sel3)


def _proj_ln_kernel(y_ref, w_ref, x_ref, gate_ref, g_ref, b_ref, o_ref, acc_ref):
    k = pl.program_id(1)

    @pl.when(k == 0)
    def _():
        acc_ref[...] = jnp.zeros(acc_ref.shape, F32)

    acc_ref[...] += jnp.dot(y_ref[...], w_ref[...], preferred_element_type=F32)

    @pl.when(k == pl.num_programs(1) - 1)
    def _():
        u = DEEPNORM_ALPHA * x_ref[...] + (1.0 + gate_ref[...]) * acc_ref[...]
        o_ref[...] = _layer_norm(u, g_ref[...], b_ref[...])


def _proj_ln(y, w, x2, gate, ln_g, ln_b, *, seq, tm, tk):
    t, kdim = y.shape
    d = w.shape[1]
    per_batch = seq // tm
    return pl.pallas_call(
        _proj_ln_kernel,
        grid=(t // tm, kdim // tk),
        in_specs=[
            pl.BlockSpec((tm, tk), lambda i, k: (i, k)),
            pl.BlockSpec((tk, d), lambda i, k: (k, 0)),
            pl.BlockSpec((tm, d), lambda i, k: (i, 0)),
            pl.BlockSpec((None, 1, d), lambda i, k: (i // per_batch, 0, 0)),
            pl.BlockSpec((1, d), lambda i, k: (0, 0)),
            pl.BlockSpec((1, d), lambda i, k: (0, 0)),
        ],
        out_specs=pl.BlockSpec((tm, d), lambda i, k: (i, 0)),
        out_shape=jax.ShapeDtypeStruct((t, d), F32),
        scratch_shapes=[pltpu.VMEM((tm, d), F32)],
        compiler_params=_params(("parallel", "arbitrary")),
        name="out_proj_ln",
    )(y, w, x2, gate, ln_g.reshape(1, d), ln_b.reshape(1, d))


def _attn_kernel(q_ref, kc_ref, kp_ref, vc_ref, vp_ref, o_ref, lse_ref, *, slopes, dilation):
    n = pl.program_id(2)
    blk = DIL_BLOCK
    qi = lax.broadcasted_iota(jnp.int32, (blk, blk), 0)
    kj = lax.broadcasted_iota(jnp.int32, (blk, blk), 1)
    delta_c = qi - kj
    delta_p = delta_c + blk
    valid_c = delta_c >= 0
    valid_p = (delta_p <= blk) & (n > 0)
    dist_c = (delta_c * dilation).astype(F32)
    dist_p = (delta_p * dilation).astype(F32)
    lane = lax.broadcasted_iota(jnp.int32, (blk, LANES), 1)
    scale = DIL_HEAD_DIM ** -0.5
    dn = (((1,), (1,)), ((), ()))
    lse_tile = jnp.zeros((blk, LANES), F32)
    for h in range(DIL_HEADS):
        cols = slice(h * DIL_HEAD_DIM, (h + 1) * DIL_HEAD_DIM)
        q = q_ref[:, cols]
        slope = jnp.float32(slopes[h])
        s_c = lax.dot_general(q, kc_ref[:, cols], dn, preferred_element_type=F32) * scale
        s_p = lax.dot_general(q, kp_ref[:, cols], dn, preferred_element_type=F32) * scale
        s_c = jnp.where(valid_c, s_c + (-slope) * dist_c, -jnp.inf)
        s_p = jnp.where(valid_p, s_p + (-slope) * dist_p, -jnp.inf)
        m = jnp.maximum(jnp.max(s_c, axis=-1, keepdims=True), jnp.max(s_p, axis=-1, keepdims=True))
        p_c = jnp.exp(s_c - m)
        p_p = jnp.exp(s_p - m)
        den = jnp.sum(p_c, axis=-1, keepdims=True) + jnp.sum(p_p, axis=-1, keepdims=True)
        o = (jnp.dot(p_c.astype(BF16), vc_ref[:, cols], preferred_element_type=F32)
             + jnp.dot(p_p.astype(BF16), vp_ref[:, cols], preferred_element_type=F32))
        o_ref[:, cols] = o / den
        lse_tile = jnp.where(lane == h, m + jnp.log(den), lse_tile)
    lse_ref[...] = lse_tile


def _dilated_attention(q, kv, group, *, batch, seq):
    window, dilation = DIL_PATTERNS[group]
    assert window // dilation == DIL_BLOCK and seq % (dilation * DIL_BLOCK) == 0
    m = seq // dilation
    nb = m // DIL_BLOCK
    w = DIL_OUT_WIDTH
    n_all = DIL_N_GROUPS * DIL_HEADS
    slopes = 2.0 ** (-8.0 * np.arange(1, n_all + 1) / n_all)
    slopes = tuple(float(np.float32(s)) for s in slopes.reshape(DIL_N_GROUPS, DIL_HEADS)[group])
    q3 = q.reshape(batch, m, dilation * DIL_Q_WIDTH)
    kv3 = kv.reshape(batch, m, dilation * 2 * DIL_Q_WIDTH)
    qb = DIL_Q_WIDTH // w
    kb = 2 * DIL_Q_WIDTH // w
    blk = DIL_BLOCK

    def prev(n):
        return jnp.maximum(n - 1, 0)

    o, lse = pl.pallas_call(
        functools.partial(_attn_kernel, slopes=slopes, dilation=dilation),
        grid=(batch, dilation, nb),
        in_specs=[
            pl.BlockSpec((None, blk, w), lambda b, r, n: (b, n, r * qb + group)),
            pl.BlockSpec((None, blk, w), lambda b, r, n: (b, n, r * kb + group)),
            pl.BlockSpec((None, blk, w), lambda b, r, n: (b, prev(n), r * kb + group)),
            pl.BlockSpec((None, blk, w), lambda b, r, n: (b, n, r * kb + DIL_N_GROUPS + group)),
            pl.BlockSpec((None, blk, w), lambda b, r, n: (b, prev(n), r * kb + DIL_N_GROUPS + group)),
        ],
        out_specs=[
            pl.BlockSpec((None, blk, w), lambda b, r, n: (b, n, r)),
            pl.BlockSpec((None, blk, LANES), lambda b, r, n: (b, n, r)),
        ],
        out_shape=[
            jax.ShapeDtypeStruct((batch, m, dilation * w), F32),
            jax.ShapeDtypeStruct((batch, m, dilation * LANES), F32),
        ],
        compiler_params=_params(("parallel", "parallel", "arbitrary")),
        name=f"dilated_attn_{group}",
    )(q3, kv3, kv3, kv3, kv3)
    return o.reshape(batch * seq, w), lse.reshape(batch * seq, LANES)


def _merge_ln_kernel(o0_ref, o1_ref, o2_ref, l0_ref, l1_ref, l2_ref, z_ref, w_ref, x_ref,
                     gate_ref, g_ref, b_ref, out_ref):
    l0, l1, l2 = l0_ref[...], l1_ref[...], l2_ref[...]
    mx = jnp.maximum(jnp.maximum(l0, l1), l2)
    e0, e1, e2 = jnp.exp(l0 - mx), jnp.exp(l1 - mx), jnp.exp(l2 - mx)
    den = e0 + e1 + e2
    w0, w1, w2 = e0 / den, e1 / den, e2 / den
    pieces = []
    for h in range(DIL_HEADS):
        cols = slice(h * DIL_HEAD_DIM, (h + 1) * DIL_HEAD_DIM)
        pieces.append(o0_ref[:, cols] * w0[:, h:h + 1]
                      + o1_ref[:, cols] * w1[:, h:h + 1]
                      + o2_ref[:, cols] * w2[:, h:h + 1])
    o = jnp.concatenate(pieces, axis=1) * _silu(z_ref[...])
    y = jnp.dot(o.astype(BF16), w_ref[...], preferred_element_type=F32)
    u = DEEPNORM_ALPHA * x_ref[...] + (1.0 + gate_ref[...]) * y
    out_ref[...] = _layer_norm(u, g_ref[...], b_ref[...])


def _merge_ln(os_, lses, z, w, x2, gate, ln_g, ln_b, *, seq, tm):
    t, d = x2.shape
    ow = DIL_OUT_WIDTH
    per_batch = seq // tm
    row = lambda i: (i, 0)
    return pl.pallas_call(
        _merge_ln_kernel,
        grid=(t // tm,),
        in_specs=[pl.BlockSpec((tm, ow), row)] * 3 + [pl.BlockSpec((tm, LANES), row)] * 3 + [
            pl.BlockSpec((tm, ow), row),
            pl.BlockSpec((ow, d), lambda i: (0, 0)),
            pl.BlockSpec((tm, d), row),
            pl.BlockSpec((None, 1, d), lambda i: (i // per_batch, 0, 0)),
            pl.BlockSpec((1, d), lambda i: (0, 0)),
            pl.BlockSpec((1, d), lambda i: (0, 0)),
        ],
        out_specs=pl.BlockSpec((tm, d), row),
        out_shape=jax.ShapeDtypeStruct((t, d), F32),
        compiler_params=_params(("parallel",)),
        name="merge_out_proj_ln",
    )(*os_, *lses, z, w, x2, gate, ln_g.reshape(1, d), ln_b.reshape(1, d))


def kernel(x, c, ada_w, ada_b, ln_g, ln_b, a_in_w, a_conv_w, a_conv_b, a_dt_bias, a_A_log, a_D,
           a_norm_g, a_out_w, kv_w, b_in_w, b_out_w):
    batch, seq, d = x.shape
    t = batch * seq
    x2 = x.reshape(t, d)

    c_pad = jnp.pad(c, ((0, 2 * SUBLANES - batch), (0, 0)))
    mod = _adaln(c_pad, ada_w, ada_b)[:, :batch]
    shift = mod[:, :, None, 0:d]
    scale = mod[:, :, None, d:2 * d]
    gate = mod[:, :, None, 2 * d:3 * d]

    zx_w = a_in_w[0, :, :SSD_D_INNER + SSD_CONV_DIM].astype(BF16)
    dt_w = jnp.pad(a_in_w[0, :, SSD_D_INNER + SSD_CONV_DIM:], ((0, 0), (0, LANES - SSD_N_HEADS))).astype(BF16)
    zx, dt = _mod_matmul(x2, scale[0], shift[0], zx_w, F32, seq=seq, modulate=True,
                         tm=1024, tn=512, wdt=dt_w, name="ssd_in_proj")
    y = _ssd_scan(zx, dt, a_conv_w[0], a_conv_b[0], a_dt_bias[0], a_A_log[0], a_D[0], a_norm_g[0],
                  batch=batch, seq=seq)
    x1 = _proj_ln(y, a_out_w[0].astype(BF16), x2, gate[0], ln_g[0], ln_b[0], seq=seq, tm=512, tk=1024)

    kv = _mod_matmul(x1, scale[1], shift[1], kv_w.astype(BF16), BF16, seq=seq, modulate=False,
                     tm=1024, tn=512, name="kv_proj")
    q = _mod_matmul(x1, scale[1], shift[1], b_in_w[0, :, :DIL_Q_WIDTH].astype(BF16), BF16, seq=seq,
                    modulate=True, tm=1024, tn=512, name="q_proj")
    z = _mod_matmul(x1, scale[1], shift[1], b_in_w[0, :, DIL_Q_WIDTH:].astype(BF16), F32, seq=seq,
                    modulate=True, tm=1024, tn=512, name="gate_proj")

    os_, lses = [], []
    for group in range(DIL_N_GROUPS):
        o, lse = _dilated_attention(q, kv, group, batch=batch, seq=seq)
        os_.append(o)
        lses.append(lse)

    out = _merge_ln(os_, lses, z, b_out_w[0].astype(BF16), x1, gate[1], ln_g[1], ln_b[1], seq=seq, tm=256)
    return out.reshape(batch, seq, d)
```

```python
import functools

import numpy as np
import jax
import jax.numpy as jnp
from jax import lax
from jax.experimental import pallas as pl
from jax.experimental.pallas import tpu as pltpu

F32 = jnp.float32
BF16 = jnp.bfloat16

D_MODEL = 2048
DEPTH = 2
LANES = 128
SUBLANES = 8

SSD_D_INNER = 2 * D_MODEL
SSD_HEAD_DIM = 64
SSD_N_HEADS = SSD_D_INNER // SSD_HEAD_DIM
SSD_N_GROUPS = 8
SSD_HEADS_PER_GROUP = SSD_N_HEADS // SSD_N_GROUPS
SSD_D_STATE = 128
SSD_CONV_W = 4
SSD_CHUNK = 256
SSD_BC_DIM = SSD_N_GROUPS * SSD_D_STATE
SSD_CONV_DIM = SSD_D_INNER + 2 * SSD_BC_DIM
SSD_GROUP_W = SSD_HEADS_PER_GROUP * SSD_HEAD_DIM

DIL_PATTERNS = ((128, 1), (512, 4), (2048, 16))
DIL_N_GROUPS = len(DIL_PATTERNS)
DIL_HEADS = 8
DIL_HEAD_DIM = 128
DIL_Q_WIDTH = DIL_N_GROUPS * DIL_HEADS * DIL_HEAD_DIM
DIL_OUT_WIDTH = DIL_HEADS * DIL_HEAD_DIM
DIL_BLOCK = 128

DEEPNORM_ALPHA = (2 * DEPTH) ** 0.25
LN_EPS = 1e-5
RMS_EPS = 1e-5

VMEM_LIMIT = 56 * 1024 * 1024


def _params(sem):
    return pltpu.CompilerParams(dimension_semantics=sem, vmem_limit_bytes=VMEM_LIMIT)


def _silu(v):
    return v * (1.0 / (1.0 + jnp.exp(-v)))


def _split3(v):
    v1 = v.astype(BF16)
    r1 = v - v1.astype(F32)
    v2 = r1.astype(BF16)
    r2 = r1 - v2.astype(F32)
    v3 = r2.astype(BF16)
    return jnp.concatenate([v1, v2, v3], axis=1)


def _layer_norm(u, g, b):
    mu = jnp.mean(u, axis=-1, keepdims=True)
    d = u - mu
    var = jnp.mean(d * d, axis=-1, keepdims=True)
    return d * lax.rsqrt(var + LN_EPS) * g + b


def _adaln_kernel(c_ref, w_ref, b_ref, o_ref):
    s = _silu(c_ref[...]).astype(BF16)
    o_ref[...] = jnp.dot(s, w_ref[...].astype(BF16), preferred_element_type=F32) + b_ref[...]


def _adaln(c_pad, ada_w, ada_b):
    rows = c_pad.shape[0]
    n = 3 * D_MODEL
    tn = 768
    return pl.pallas_call(
        _adaln_kernel,
        grid=(DEPTH, n // tn),
        in_specs=[
            pl.BlockSpec((rows, D_MODEL), lambda l, j: (0, 0)),
            pl.BlockSpec((None, D_MODEL, tn), lambda l, j: (l, 0, j)),
            pl.BlockSpec((None, 1, tn), lambda l, j: (l, 0, j)),
        ],
        out_specs=pl.BlockSpec((None, rows, tn), lambda l, j: (l, 0, j)),
        out_shape=jax.ShapeDtypeStruct((DEPTH, rows, n), F32),
        compiler_params=_params(("parallel", "parallel")),
        name="adaln",
    )(c_pad, ada_w, ada_b.reshape(DEPTH, 1, n))


def _mm_kernel(*refs, modulate, with_dt):
    if with_dt:
        x_ref, scale_ref, shift_ref, w_ref, wdt_ref, o_ref, dt_ref, h_ref = refs
    else:
        x_ref, scale_ref, shift_ref, w_ref, o_ref, h_ref = refs

    @pl.when(pl.program_id(1) == 0)
    def _():
        x = x_ref[...]
        if modulate:
            x = x * (1.0 + scale_ref[...]) + shift_ref[...]
        h = x.astype(BF16)
        h_ref[...] = h
        if with_dt:
            dt_ref[...] = jnp.dot(h, wdt_ref[...], preferred_element_type=F32)

    o_ref[...] = jnp.dot(h_ref[...], w_ref[...], preferred_element_type=F32).astype(o_ref.dtype)


def _mod_matmul(x2, scale, shift, w, out_dtype, *, seq, modulate, tm, tn, wdt=None, name):
    t, d = x2.shape
    n = w.shape[1]
    per_batch = seq // tm
    with_dt = wdt is not None
    in_specs = [
        pl.BlockSpec((tm, d), lambda i, j: (i, 0)),
        pl.BlockSpec((None, 1, d), lambda i, j: (i // per_batch, 0, 0)),
        pl.BlockSpec((None, 1, d), lambda i, j: (i // per_batch, 0, 0)),
        pl.BlockSpec((d, tn), lambda i, j: (0, j)),
    ]
    args = [x2, scale, shift, w]
    out_specs = [pl.BlockSpec((tm, tn), lambda i, j: (i, j))]
    out_shape = [jax.ShapeDtypeStruct((t, n), out_dtype)]
    if with_dt:
        in_specs.append(pl.BlockSpec((d, LANES), lambda i, j: (0, 0)))
        args.append(wdt)
        out_specs.append(pl.BlockSpec((tm, LANES), lambda i, j: (i, 0)))
        out_shape.append(jax.ShapeDtypeStruct((t, LANES), F32))
    res = pl.pallas_call(
        functools.partial(_mm_kernel, modulate=modulate, with_dt=with_dt),
        grid=(t // tm, n // tn),
        in_specs=in_specs,
        out_specs=out_specs,
        out_shape=out_shape,
        scratch_shapes=[pltpu.VMEM((tm, d), BF16)],
        compiler_params=_params(("parallel", "arbitrary")),
        name=name,
    )(*args)
    return res if with_dt else res[0]


def _conv_silu(raw, carry, w, b):
    acc = raw * w[SSD_CONV_W - 1:SSD_CONV_W, :] + b
    row = lax.broadcasted_iota(jnp.int32, carry.shape, 0)
    for s in range(1, SSD_CONV_W):
        sh = pltpu.roll(raw, s, axis=0)
        prev = pltpu.roll(carry, s, axis=0)
        head = jnp.where(row < s, prev, sh[:SUBLANES])
        sh = jnp.concatenate([head, sh[SUBLANES:]], axis=0)
        acc = acc + sh * w[SSD_CONV_W - 1 - s:SSD_CONV_W - s, :]
    return _silu(acc)


def _ssd_kernel(z_ref, x_ref, b_ref, c_ref, dt_ref,
                cwx_ref, cwb_ref, cwc_ref, cbx_ref, cbb_ref, cbc_ref,
                dtb_ref, alog_ref, dexp_ref, ng_ref, tril_ref, e_ref, sel_ref,
                y_ref,
                state_ref, carx_ref, carb_ref, carc_ref, v3_ref, ac3_ref):
    c = pl.program_id(1)
    g = pl.program_id(2)
    L = SSD_CHUNK

    @pl.when(c == 0)
    def _():
        state_ref[g] = jnp.zeros(state_ref.shape[1:], F32)
        carx_ref[g] = jnp.zeros(carx_ref.shape[1:], F32)
        carb_ref[g] = jnp.zeros(carb_ref.shape[1:], F32)
        carc_ref[g] = jnp.zeros(carc_ref.shape[1:], F32)

    @pl.when(g == 0)
    def _():
        v = dt_ref[...] + dtb_ref[...]
        dt = jnp.maximum(v, 0.0) + jnp.log1p(jnp.exp(-jnp.abs(v)))
        a = dt * (-jnp.exp(alog_ref[...]))
        a3 = _split3(a)
        tril = tril_ref[...]
        acum = (jnp.dot(tril, a3[:, :LANES], preferred_element_type=F32)
                + jnp.dot(tril, a3[:, LANES:2 * LANES], preferred_element_type=F32)
                + jnp.dot(tril, a3[:, 2 * LANES:], preferred_element_type=F32))
        tail = jnp.exp(acum[L - 1:L, :] - acum)
        v3_ref[0:L, :] = _split3(dt)
        v3_ref[L:2 * L, :] = _split3(tail)
        v3_ref[2 * L:3 * L, :] = _split3(jnp.exp(acum))
        ac3_ref[...] = _split3(acum)

    ex = jnp.dot(v3_ref[...], e_ref[...], preferred_element_type=F32)
    dt_e = ex[0:L]
    tail_e = ex[L:2 * L]
    eac_e = ex[2 * L:3 * L]
    acg = jnp.dot(ac3_ref[...], sel_ref[...], preferred_element_type=F32)
    acg_t = acg.T

    x_raw = x_ref[...]
    b_raw = b_ref[...]
    c_raw = c_ref[...]
    xg = _conv_silu(x_raw, carx_ref[g], cwx_ref[...], cbx_ref[...])
    bg = _conv_silu(b_raw, carb_ref[g], cwb_ref[...], cbb_ref[...])
    cg = _conv_silu(c_raw, carc_ref[g], cwc_ref[...], cbc_ref[...])
    carx_ref[g] = x_raw[L - SUBLANES:, :]
    carb_ref[g] = b_raw[L - SUBLANES:, :]
    carc_ref[g] = c_raw[L - SUBLANES:, :]

    xdt = xg * dt_e
    cb16 = cg.astype(BF16)
    bb16 = bg.astype(BF16)
    cb = lax.dot_general(cb16, bb16, (((1,), (1,)), ((), ())), preferred_element_type=F32)

    state = state_ref[g]
    y_off = jnp.dot(cb16, state.astype(BF16), preferred_element_type=F32) * eac_e

    li = lax.broadcasted_iota(jnp.int32, (L, L), 0)
    si = lax.broadcasted_iota(jnp.int32, (L, L), 1)
    causal = li >= si
    lane = lax.broadcasted_iota(jnp.int32, (L, LANES), 1)
    heads_per_tile = LANES // SSD_HEAD_DIM
    pieces = []
    for pair in range(SSD_GROUP_W // LANES):
        xp = xdt[:, pair * LANES:(pair + 1) * LANES]
        acc = None
        for sub in range(heads_per_tile):
            k = pair * heads_per_tile + sub
            seg = acg[:, k:k + 1] - acg_t[k:k + 1, :]
            decay = jnp.exp(jnp.where(causal, seg, -jnp.inf))
            m = (cb * decay).astype(BF16)
            in_head = (lane >= sub * SSD_HEAD_DIM) & (lane < (sub + 1) * SSD_HEAD_DIM)
            xm = jnp.where(in_head, xp, 0.0).astype(BF16)
            part = jnp.dot(m, xm, preferred_element_type=F32)
            acc = part if acc is None else acc + part
        pieces.append(acc)
    y = jnp.concatenate(pieces, axis=1) + y_off + xg * dexp_ref[...]

    xtail = (xdt * tail_e).astype(BF16)
    upd = lax.dot_general(bb16, xtail, (((0,), (0,)), ((), ())), preferred_element_type=F32)
    state_ref[g] = state * eac_e[L - 1:L, :] + upd

    y = y * _silu(z_ref[...])
    y = y * lax.rsqrt(jnp.mean(y * y, axis=-1, keepdims=True) + RMS_EPS)
    y_ref[...] = (y * ng_ref[...]).astype(y_ref.dtype)


def _ssd_constants():
    tril = np.tril(np.ones((SSD_CHUNK, SSD_CHUNK), np.float32))
    head_of_col = np.arange(SSD_D_INNER) // SSD_HEAD_DIM
    e = (np.arange(LANES)[:, None] == head_of_col[None, :]).astype(np.float32)
    col = np.arange(SSD_N_GROUPS * LANES)
    src = np.where(col % LANES < SSD_HEADS_PER_GROUP,
                   (col // LANES) * SSD_HEADS_PER_GROUP + col % LANES, -1)
    sel = (np.arange(LANES)[:, None] == src[None, :]).astype(np.float32)
    e3 = np.concatenate([e, e, e], axis=0)
    sel3 = np.concatenate([sel, sel, sel], axis=0)
    return jnp.asarray(tril, BF16), jnp.asarray(e3, BF16), jnp.asarray(sel3, BF16)


def _ssd_scan(zx, dt, conv_w, conv_b, dt_bias, a_log, d_skip, norm_g, *, batch, seq):
    t = zx.shape[0]
    L = SSD_CHUNK
    nc = seq // L
    G = SSD_N_GROUPS
    gw = SSD_GROUP_W
    ns = SSD_D_STATE
    tril, e3, sel3 = _ssd_constants()
    pad = LANES - SSD_N_HEADS
    dtb = jnp.pad(dt_bias, (0, pad)).reshape(1, LANES)
    alog = jnp.pad(a_log, (0, pad)).reshape(1, LANES)
    dexp = jnp.repeat(d_skip, SSD_HEAD_DIM).reshape(1, SSD_D_INNER)
    ng = norm_g.reshape(1, SSD_D_INNER)
    cb2 = conv_b.reshape(1, SSD_CONV_DIM)

    x_off = SSD_D_INNER // gw
    b_off = (2 * SSD_D_INNER) // ns
    c_off = (2 * SSD_D_INNER + SSD_BC_DIM) // ns
    wb_off = SSD_D_INNER // ns
    wc_off = (SSD_D_INNER + SSD_BC_DIM) // ns

    def row(b, c, g):
        return b * nc + c

    in_specs = [
        pl.BlockSpec((L, gw), lambda b, c, g: (row(b, c, g), g)),
        pl.BlockSpec((L, gw), lambda b, c, g: (row(b, c, g), x_off + g)),
        pl.BlockSpec((L, ns), lambda b, c, g: (row(b, c, g), b_off + g)),
        pl.BlockSpec((L, ns), lambda b, c, g: (row(b, c, g), c_off + g)),
        pl.BlockSpec((L, LANES), lambda b, c, g: (row(b, c, g), 0)),
        pl.BlockSpec((SSD_CONV_W, gw), lambda b, c, g: (0, g)),
        pl.BlockSpec((SSD_CONV_W, ns), lambda b, c, g: (0, wb_off + g)),
        pl.BlockSpec((SSD_CONV_W, ns), lambda b, c, g: (0, wc_off + g)),
        pl.BlockSpec((1, gw), lambda b, c, g: (0, g)),
        pl.BlockSpec((1, ns), lambda b, c, g: (0, wb_off + g)),
        pl.BlockSpec((1, ns), lambda b, c, g: (0, wc_off + g)),
        pl.BlockSpec((1, LANES), lambda b, c, g: (0, 0)),
        pl.BlockSpec((1, LANES), lambda b, c, g: (0, 0)),
        pl.BlockSpec((1, gw), lambda b, c, g: (0, g)),
        pl.BlockSpec((1, gw), lambda b, c, g: (0, g)),
        pl.BlockSpec((L, L), lambda b, c, g: (0, 0)),
        pl.BlockSpec((3 * LANES, gw), lambda b, c, g: (0, g)),
        pl.BlockSpec((3 * LANES, LANES), lambda b, c, g: (0, g)),
    ]
    return pl.pallas_call(
        _ssd_kernel,
        grid=(batch, nc, G),
        in_specs=in_specs,
        out_specs=pl.BlockSpec((L, gw), lambda b, c, g: (row(b, c, g), g)),
        out_shape=jax.ShapeDtypeStruct((t, SSD_D_INNER), BF16),
        scratch_shapes=[
            pltpu.VMEM((G, ns, gw), F32),
            pltpu.VMEM((G, SUBLANES, gw), F32),
            pltpu.VMEM((G, SUBLANES, ns), F32),
            pltpu.VMEM((G, SUBLANES, ns), F32),
            pltpu.VMEM((3 * L, 3 * LANES), BF16),
            pltpu.VMEM((L, 3 * LANES), BF16),
        ],
        compiler_params=_params(("parallel", "arbitrary", "arbitrary")),
        name="ssd_scan",
    )(zx, zx, zx, zx, dt, conv_w, conv_w, conv_w, cb2, cb2, cb2, dtb, alog, dexp, ng, tril, e3, sel3)


def _proj_ln_kernel(y_ref, w_ref, x_ref, gate_ref, g_ref, b_ref, o_ref, acc_ref):
    k = pl.program_id(1)

    @pl.when(k == 0)
    def _():
        acc_ref[...] = jnp.zeros(acc_ref.shape, F32)

    acc_ref[...] += jnp.dot(y_ref[...], w_ref[...], preferred_element_type=F32)

    @pl.when(k == pl.num_programs(1) - 1)
    def _():
        u = DEEPNORM_ALPHA * x_ref[...] + (1.0 + gate_ref[...]) * acc_ref[...]
        o_ref[...] = _layer_norm(u, g_ref[...], b_ref[...])


def _proj_ln(y, w, x2, gate, ln_g, ln_b, *, seq, tm, tk):
    t, kdim = y.shape
    d = w.shape[1]
    per_batch = seq // tm
    return pl.pallas_call(
        _proj_ln_kernel,
        grid=(t // tm, kdim // tk),
        in_specs=[
            pl.BlockSpec((tm, tk), lambda i, k: (i, k)),
            pl.BlockSpec((tk, d), lambda i, k: (k, 0)),
            pl.BlockSpec((tm, d), lambda i, k: (i, 0)),
            pl.BlockSpec((None, 1, d), lambda i, k: (i // per_batch, 0, 0)),
            pl.BlockSpec((1, d), lambda i, k: (0, 0)),
            pl.BlockSpec((1, d), lambda i, k: (0, 0)),
        ],
        out_specs=pl.BlockSpec((tm, d), lambda i, k: (i, 0)),
        out_shape=jax.ShapeDtypeStruct((t, d), F32),
        scratch_shapes=[pltpu.VMEM((tm, d), F32)],
        compiler_params=_params(("parallel", "arbitrary")),
        name="out_proj_ln",
    )(y, w, x2, gate, ln_g.reshape(1, d), ln_b.reshape(1, d))


def _b_proj_kernel(x_ref, scale_ref, shift_ref, w_ref, o0_ref, o1_ref, o2_ref, z_ref, h_ref, res_ref):
    j = pl.program_id(1)
    tm = x_ref.shape[0]
    n_lane_tiles = w_ref.shape[1] // LANES
    n_qkv = 3 * DIL_N_GROUPS

    @pl.when(j == 0)
    def _():
        x = x_ref[...]
        h_ref[0] = (x * (1.0 + scale_ref[...]) + shift_ref[...]).astype(BF16)
        h_ref[1] = x.astype(BF16)

    plain = ((j >= DIL_N_GROUPS) & (j < n_qkv)).astype(jnp.int32)
    acc = jnp.dot(h_ref[plain], w_ref[...], preferred_element_type=F32)
    for ct in range(n_lane_tiles):
        res_ref[ct] = acc[:, ct * LANES:(ct + 1) * LANES]

    for group, out_ref in enumerate((o0_ref, o1_ref, o2_ref)):
        dilation = DIL_PATTERNS[group][1]
        rows = tm // dilation

        @pl.when((j < n_qkv) & (j % DIL_N_GROUPS == group))
        def _(out_ref=out_ref, dilation=dilation, rows=rows):
            for r in range(dilation):
                for ct in range(n_lane_tiles):
                    out_ref[r, :, ct * LANES:(ct + 1) * LANES] = (
                        res_ref[ct, pl.ds(r, rows, stride=dilation), :].astype(BF16))

    @pl.when(j == n_qkv)
    def _():
        for ct in range(n_lane_tiles):
            z_ref[:, ct * LANES:(ct + 1) * LANES] = res_ref[ct]


def _b_proj(x2, scale, shift, w, *, batch, seq, tm):
    t, d = x2.shape
    ow = DIL_OUT_WIDTH
    per_batch = seq // tm
    n_qkv = 3 * DIL_N_GROUPS
    out_specs, out_shape = [], []
    for _, dilation in DIL_PATTERNS:
        out_specs.append(pl.BlockSpec(
            (None, dilation, tm // dilation, ow),
            lambda i, j: (i // per_batch, 0, i % per_batch, jnp.minimum(j // DIL_N_GROUPS, 2))))
        out_shape.append(jax.ShapeDtypeStruct((batch, dilation, seq // dilation, 3 * ow), BF16))
    out_specs.append(pl.BlockSpec((tm, ow), lambda i, j: (i, 0)))
    out_shape.append(jax.ShapeDtypeStruct((t, ow), F32))
    return pl.pallas_call(
        _b_proj_kernel,
        grid=(t // tm, n_qkv + 1),
        in_specs=[
            pl.BlockSpec((tm, d), lambda i, j: (i, 0)),
            pl.BlockSpec((None, 1, d), lambda i, j: (i // per_batch, 0, 0)),
            pl.BlockSpec((None, 1, d), lambda i, j: (i // per_batch, 0, 0)),
            pl.BlockSpec((d, ow), lambda i, j: (0, j)),
        ],
        out_specs=out_specs,
        out_shape=out_shape,
        scratch_shapes=[pltpu.VMEM((2, tm, d), BF16), pltpu.VMEM((ow // LANES, tm, LANES), F32)],
        compiler_params=_params(("parallel", "arbitrary")),
        name="dilated_in_proj",
    )(x2, scale, shift, w)


def _attn_bias(group):
    _, dilation = DIL_PATTERNS[group]
    n_all = DIL_N_GROUPS * DIL_HEADS
    slopes = 2.0 ** (-8.0 * np.arange(1, n_all + 1) / n_all)
    slopes = slopes.reshape(DIL_N_GROUPS, DIL_HEADS)[group].astype(np.float32)
    qi = np.arange(DIL_BLOCK)[:, None]
    kj = np.arange(2 * DIL_BLOCK)[None, :]
    delta = qi + DIL_BLOCK - kj
    valid = (delta >= 0) & (delta <= DIL_BLOCK)
    alibi = -slopes[:, None, None] * (delta * dilation).astype(np.float32)[None]
    return jnp.asarray(np.where(valid[None], alibi, -np.inf).astype(np.float32))


def _attn_kernel(q_ref, k_ref, v_ref, kp_ref, vp_ref, bias_ref, o_ref, lse_ref):
    first = pl.program_id(2) == 0
    blk = DIL_BLOCK
    tq = q_ref.shape[0]
    scale = DIL_HEAD_DIM ** -0.5
    dn = (((1,), (1,)), ((), ()))
    lane = lax.broadcasted_iota(jnp.int32, (blk, LANES), 1)
    kcol = lax.broadcasted_iota(jnp.int32, (blk, 2 * blk), 1)
    no_prev = first & (kcol < blk)
    for sb in range(tq // blk):
        rows = slice(sb * blk, (sb + 1) * blk)
        lse_tile = jnp.zeros((blk, LANES), F32)
        for h in range(DIL_HEADS):
            cols = slice(h * DIL_HEAD_DIM, (h + 1) * DIL_HEAD_DIM)
            q = q_ref[rows, cols]
            bias = bias_ref[h]
            if sb == 0:
                s = jnp.concatenate(
                    [lax.dot_general(q, kp_ref[:, cols], dn, preferred_element_type=F32),
                     lax.dot_general(q, k_ref[rows, cols], dn, preferred_element_type=F32)], axis=1)
                bias = jnp.where(no_prev, -jnp.inf, bias)
            else:
                keys = slice((sb - 1) * blk, (sb + 1) * blk)
                s = lax.dot_general(q, k_ref[keys, cols], dn, preferred_element_type=F32)
            s = s * scale + bias
            m = jnp.max(s, axis=-1, keepdims=True)
            p = jnp.exp(s - m)
            den = jnp.sum(p, axis=-1, keepdims=True)
            p16 = p.astype(BF16)
            if sb == 0:
                o = (jnp.dot(p16[:, :blk], vp_ref[:, cols], preferred_element_type=F32)
                     + jnp.dot(p16[:, blk:], v_ref[rows, cols], preferred_element_type=F32))
            else:
                o = jnp.dot(p16, v_ref[keys, cols], preferred_element_type=F32)
            o_ref[rows, cols] = o * (1.0 / den)
            lse_tile = jnp.where(lane == h, m + jnp.log(den), lse_tile)
        lse_ref[rows, :] = lse_tile


def _dilated_attention(qkv, group, *, tq):
    batch, dilation, m, _ = qkv.shape
    assert DIL_PATTERNS[group][0] // dilation == DIL_BLOCK and m % tq == 0
    w = DIL_OUT_WIDTH
    blk = DIL_BLOCK
    sub = tq // blk

    def prev(i):
        return jnp.maximum(i * sub - 1, 0)

    return pl.pallas_call(
        _attn_kernel,
        grid=(batch, dilation, m // tq),
        in_specs=[
            pl.BlockSpec((None, None, tq, w), lambda b, r, i: (b, r, i, 0)),
            pl.BlockSpec((None, None, tq, w), lambda b, r, i: (b, r, i, 1)),
            pl.BlockSpec((None, None, tq, w), lambda b, r, i: (b, r, i, 2)),
            pl.BlockSpec((None, None, blk, w), lambda b, r, i: (b, r, prev(i), 1)),
            pl.BlockSpec((None, None, blk, w), lambda b, r, i: (b, r, prev(i), 2)),
            pl.BlockSpec((DIL_HEADS, blk, 2 * blk), lambda b, r, i: (0, 0, 0)),
        ],
        out_specs=[
            pl.BlockSpec((None, None, tq, w), lambda b, r, i: (b, r, i, 0)),
            pl.BlockSpec((None, None, tq, LANES), lambda b, r, i: (b, r, i, 0)),
        ],
        out_shape=[
            jax.ShapeDtypeStruct((batch, dilation, m, w), F32),
            jax.ShapeDtypeStruct((batch, dilation, m, LANES), F32),
        ],
        compiler_params=_params(("parallel", "parallel", "arbitrary")),
        name=f"dilated_attn_{group}",
    )(qkv, qkv, qkv, qkv, qkv, _attn_bias(group))


def _merge_ln_kernel(o0_ref, o1_ref, o2_ref, l0_ref, l1_ref, l2_ref, z_ref, w_ref, x_ref,
                     gate_ref, g_ref, b_ref, out_ref, on_ref, ln_ref):
    tm = x_ref.shape[0]
    assert DIL_PATTERNS[0][1] == 1
    for g, (o_ref, l_ref) in ((1, (o1_ref, l1_ref)), (2, (o2_ref, l2_ref))):
        dilation = DIL_PATTERNS[g][1]
        rows = tm // dilation
        for r in range(dilation):
            ln_ref[g - 1, pl.ds(r, rows, stride=dilation), :] = l_ref[r]
            for h in range(DIL_HEADS):
                on_ref[g - 1, h, pl.ds(r, rows, stride=dilation), :] = (
                    o_ref[r, :, h * DIL_HEAD_DIM:(h + 1) * DIL_HEAD_DIM])
    l0, l1, l2 = l0_ref[0], ln_ref[0], ln_ref[1]
    mx = jnp.maximum(jnp.maximum(l0, l1), l2)
    e0, e1, e2 = jnp.exp(l0 - mx), jnp.exp(l1 - mx), jnp.exp(l2 - mx)
    den = e0 + e1 + e2
    w0, w1, w2 = e0 / den, e1 / den, e2 / den
    pieces = []
    for h in range(DIL_HEADS):
        pieces.append(o0_ref[0, :, h * DIL_HEAD_DIM:(h + 1) * DIL_HEAD_DIM] * w0[:, h:h + 1]
                      + on_ref[0, h] * w1[:, h:h + 1]
                      + on_ref[1, h] * w2[:, h:h + 1])
    o = jnp.concatenate(pieces, axis=1) * _silu(z_ref[...])
    y = jnp.dot(o.astype(BF16), w_ref[...], preferred_element_type=F32)
    u = DEEPNORM_ALPHA * x_ref[...] + (1.0 + gate_ref[...]) * y
    out_ref[...] = _layer_norm(u, g_ref[...], b_ref[...])


def _merge_ln(os_, lses, z, w, x2, gate, ln_g, ln_b, *, seq, tm):
    t, d = x2.shape
    ow = DIL_OUT_WIDTH
    per_batch = seq // tm
    row = lambda i: (i, 0)

    def sub_major(width):
        return [pl.BlockSpec((None, dilation, tm // dilation, width),
                             lambda i: (i // per_batch, 0, i % per_batch, 0))
                for _, dilation in DIL_PATTERNS]

    return pl.pallas_call(
        _merge_ln_kernel,
        grid=(t // tm,),
        in_specs=sub_major(ow) + sub_major(LANES) + [
            pl.BlockSpec((tm, ow), row),
            pl.BlockSpec((ow, d), lambda i: (0, 0)),
            pl.BlockSpec((tm, d), row),
            pl.BlockSpec((None, 1, d), lambda i: (i // per_batch, 0, 0)),
            pl.BlockSpec((1, d), lambda i: (0, 0)),
            pl.BlockSpec((1, d), lambda i: (0, 0)),
        ],
        out_specs=pl.BlockSpec((tm, d), row),
        out_shape=jax.ShapeDtypeStruct((t, d), F32),
        scratch_shapes=[pltpu.VMEM((DIL_N_GROUPS - 1, DIL_HEADS, tm, LANES), F32),
                        pltpu.VMEM((DIL_N_GROUPS - 1, tm, LANES), F32)],
        compiler_params=_params(("parallel",)),
        name="merge_out_proj_ln",
    )(*os_, *lses, z, w, x2, gate, ln_g.reshape(1, d), ln_b.reshape(1, d))


def kernel(x, c, ada_w, ada_b, ln_g, ln_b, a_in_w, a_conv_w, a_conv_b, a_dt_bias, a_A_log, a_D,
           a_norm_g, a_out_w, kv_w, b_in_w, b_out_w):
    batch, seq, d = x.shape
    t = batch * seq
    x2 = x.reshape(t, d)

    c_pad = jnp.pad(c, ((0, 2 * SUBLANES - batch), (0, 0)))
    mod = _adaln(c_pad, ada_w, ada_b)[:, :batch]
    shift = mod[:, :, None, 0:d]
    scale = mod[:, :, None, d:2 * d]
    gate = mod[:, :, None, 2 * d:3 * d]

    zx_w = a_in_w[0, :, :SSD_D_INNER + SSD_CONV_DIM].astype(BF16)
    dt_w = jnp.pad(a_in_w[0, :, SSD_D_INNER + SSD_CONV_DIM:], ((0, 0), (0, LANES - SSD_N_HEADS))).astype(BF16)
    zx, dt = _mod_matmul(x2, scale[0], shift[0], zx_w, F32, seq=seq, modulate=True,
                         tm=1024, tn=1024, wdt=dt_w, name="ssd_in_proj")
    y = _ssd_scan(zx, dt, a_conv_w[0], a_conv_b[0], a_dt_bias[0], a_A_log[0], a_D[0], a_norm_g[0],
                  batch=batch, seq=seq)
    x1 = _proj_ln(y, a_out_w[0].astype(BF16), x2, gate[0], ln_g[0], ln_b[0], seq=seq, tm=512, tk=1024)

    b_w = jnp.concatenate([b_in_w[0, :, :DIL_Q_WIDTH], kv_w, b_in_w[0, :, DIL_Q_WIDTH:]], axis=1).astype(BF16)
    *qkvs, z = _b_proj(x1, scale[1], shift[1], b_w, batch=batch, seq=seq, tm=512)

    os_, lses = [], []
    for group in range(DIL_N_GROUPS):
        o, lse = _dilated_attention(qkvs[group], group, tq=256)
        os_.append(o)
        lses.append(lse)

    out = _merge_ln(os_, lses, z, b_out_w[0].astype(BF16), x1, gate[1], ln_g[1], ln_b[1], seq=seq, tm=256)
    return out.reshape(batch, seq, d)
```

```python
import functools

import numpy as np
import jax
import jax.numpy as jnp
from jax import lax
from jax.experimental import pallas as pl
from jax.experimental.pallas import tpu as pltpu

F32 = jnp.float32
BF16 = jnp.bfloat16

D_MODEL = 2048
DEPTH = 2
LANES = 128
SUBLANES = 8

SSD_D_INNER = 2 * D_MODEL
SSD_HEAD_DIM = 64
SSD_N_HEADS = SSD_D_INNER // SSD_HEAD_DIM
SSD_N_GROUPS = 8
SSD_HEADS_PER_GROUP = SSD_N_HEADS // SSD_N_GROUPS
SSD_D_STATE = 128
SSD_CONV_W = 4
SSD_CHUNK = 256
SSD_BC_DIM = SSD_N_GROUPS * SSD_D_STATE
SSD_CONV_DIM = SSD_D_INNER + 2 * SSD_BC_DIM
SSD_GROUP_W = SSD_HEADS_PER_GROUP * SSD_HEAD_DIM

DIL_PATTERNS = ((128, 1), (512, 4), (2048, 16))
DIL_N_GROUPS = len(DIL_PATTERNS)
DIL_HEADS = 8
DIL_HEAD_DIM = 128
DIL_Q_WIDTH = DIL_N_GROUPS * DIL_HEADS * DIL_HEAD_DIM
DIL_OUT_WIDTH = DIL_HEADS * DIL_HEAD_DIM
DIL_BLOCK = 128

DEEPNORM_ALPHA = (2 * DEPTH) ** 0.25
LN_EPS = 1e-5
RMS_EPS = 1e-5
LOG2E = 1.4426950408889634

VMEM_LIMIT = 56 * 1024 * 1024


def _params(sem):
    return pltpu.CompilerParams(dimension_semantics=sem, vmem_limit_bytes=VMEM_LIMIT)


def _silu(v):
    return v * (1.0 / (1.0 + jnp.exp(-v)))


def _split3(v):
    v1 = v.astype(BF16)
    r1 = v - v1.astype(F32)
    v2 = r1.astype(BF16)
    r2 = r1 - v2.astype(F32)
    v3 = r2.astype(BF16)
    return jnp.concatenate([v1, v2, v3], axis=1)


def _layer_norm(u, g, b):
    mu = jnp.mean(u, axis=-1, keepdims=True)
    d = u - mu
    var = jnp.mean(d * d, axis=-1, keepdims=True)
    return d * lax.rsqrt(var + LN_EPS) * g + b


def _adaln_kernel(c_ref, w_ref, b_ref, o_ref):
    s = _silu(c_ref[...]).astype(BF16)
    o_ref[...] = jnp.dot(s, w_ref[...].astype(BF16), preferred_element_type=F32) + b_ref[...]


def _adaln(c_pad, ada_w, ada_b):
    rows = c_pad.shape[0]
    n = 3 * D_MODEL
    tn = 768
    return pl.pallas_call(
        _adaln_kernel,
        grid=(DEPTH, n // tn),
        in_specs=[
            pl.BlockSpec((rows, D_MODEL), lambda l, j: (0, 0)),
            pl.BlockSpec((None, D_MODEL, tn), lambda l, j: (l, 0, j)),
            pl.BlockSpec((None, 1, tn), lambda l, j: (l, 0, j)),
        ],
        out_specs=pl.BlockSpec((None, rows, tn), lambda l, j: (l, 0, j)),
        out_shape=jax.ShapeDtypeStruct((DEPTH, rows, n), F32),
        compiler_params=_params(("parallel", "parallel")),
        name="adaln",
    )(c_pad, ada_w, ada_b.reshape(DEPTH, 1, n))


def _mm_kernel(*refs, modulate, with_dt):
    if with_dt:
        x_ref, scale_ref, shift_ref, w_ref, wdt_ref, o_ref, dt_ref, h_ref = refs
    else:
        x_ref, scale_ref, shift_ref, w_ref, o_ref, h_ref = refs

    @pl.when(pl.program_id(1) == 0)
    def _():
        x = x_ref[...]
        if modulate:
            x = x * (1.0 + scale_ref[...]) + shift_ref[...]
        h = x.astype(BF16)
        h_ref[...] = h
        if with_dt:
            dt_ref[...] = jnp.dot(h, wdt_ref[...], preferred_element_type=F32)

    o_ref[...] = jnp.dot(h_ref[...], w_ref[...], preferred_element_type=F32).astype(o_ref.dtype)


def _mod_matmul(x2, scale, shift, w, out_dtype, *, seq, modulate, tm, tn, wdt=None, name):
    t, d = x2.shape
    n = w.shape[1]
    per_batch = seq // tm
    with_dt = wdt is not None
    in_specs = [
        pl.BlockSpec((tm, d), lambda i, j: (i, 0)),
        pl.BlockSpec((None, 1, d), lambda i, j: (i // per_batch, 0, 0)),
        pl.BlockSpec((None, 1, d), lambda i, j: (i // per_batch, 0, 0)),
        pl.BlockSpec((d, tn), lambda i, j: (0, j)),
    ]
    args = [x2, scale, shift, w]
    out_specs = [pl.BlockSpec((tm, tn), lambda i, j: (i, j))]
    out_shape = [jax.ShapeDtypeStruct((t, n), out_dtype)]
    if with_dt:
        in_specs.append(pl.BlockSpec((d, LANES), lambda i, j: (0, 0)))
        args.append(wdt)
        out_specs.append(pl.BlockSpec((tm, LANES), lambda i, j: (i, 0)))
        out_shape.append(jax.ShapeDtypeStruct((t, LANES), F32))
    res = pl.pallas_call(
        functools.partial(_mm_kernel, modulate=modulate, with_dt=with_dt),
        grid=(t // tm, n // tn),
        in_specs=in_specs,
        out_specs=out_specs,
        out_shape=out_shape,
        scratch_shapes=[pltpu.VMEM((tm, d), BF16)],
        compiler_params=_params(("parallel", "arbitrary")),
        name=name,
    )(*args)
    return res if with_dt else res[0]


def _conv_silu(raw, carry, w, b):
    wh = 0.5 * w
    acc = raw * wh[SSD_CONV_W - 1:SSD_CONV_W, :] + 0.5 * b
    row = lax.broadcasted_iota(jnp.int32, carry.shape, 0)
    for s in range(1, SSD_CONV_W):
        sh = pltpu.roll(raw, s, axis=0)
        prev = pltpu.roll(carry, s, axis=0)
        head = jnp.where(row < s, prev, sh[:SUBLANES])
        sh = jnp.concatenate([head, sh[SUBLANES:]], axis=0)
        acc = acc + sh * wh[SSD_CONV_W - 1 - s:SSD_CONV_W - s, :]
    return acc + acc * jnp.tanh(acc)


def _ssd_kernel(z_ref, x_ref, b_ref, c_ref, dt_ref,
                cwx_ref, cwb_ref, cwc_ref, cbx_ref, cbb_ref, cbc_ref,
                dtb_ref, alog_ref, dexp_ref, ng_ref, tril_ref, e_ref, sel_ref,
                ow_ref, res_ref, gate_ref, lng_ref, lnb_ref,
                out_ref,
                state_ref, carx_ref, carb_ref, carc_ref, v3_ref, ac3_ref, acc_ref):
    c = pl.program_id(1)
    g = pl.program_id(2)
    L = SSD_CHUNK

    @pl.when(c == 0)
    def _():
        state_ref[g] = jnp.zeros(state_ref.shape[1:], F32)
        carx_ref[g] = jnp.zeros(carx_ref.shape[1:], F32)
        carb_ref[g] = jnp.zeros(carb_ref.shape[1:], F32)
        carc_ref[g] = jnp.zeros(carc_ref.shape[1:], F32)

    @pl.when(g == 0)
    def _():
        v = dt_ref[...] + dtb_ref[...]
        dt = jnp.maximum(v, 0.0) + jnp.log1p(jnp.exp(-jnp.abs(v)))
        a = dt * (-jnp.exp(alog_ref[...]))
        a3 = _split3(a)
        tril = tril_ref[...]
        acum = (jnp.dot(tril, a3[:, :LANES], preferred_element_type=F32)
                + jnp.dot(tril, a3[:, LANES:2 * LANES], preferred_element_type=F32)
                + jnp.dot(tril, a3[:, 2 * LANES:], preferred_element_type=F32))
        tail = jnp.exp(acum[L - 1:L, :] - acum)
        v3_ref[0:L, :] = _split3(dt)
        v3_ref[L:2 * L, :] = _split3(tail)
        v3_ref[2 * L:3 * L, :] = _split3(jnp.exp(acum))
        ac3_ref[...] = _split3(acum)
        acc_ref[...] = jnp.zeros(acc_ref.shape, F32)

    ex = jnp.dot(v3_ref[...], e_ref[...], preferred_element_type=F32)
    dt_e = ex[0:L]
    tail_e = ex[L:2 * L]
    eac_e = ex[2 * L:3 * L]
    acg = jnp.dot(ac3_ref[...], sel_ref[...], preferred_element_type=F32) * LOG2E
    acg_t = acg.T

    x_raw = x_ref[...]
    b_raw = b_ref[...]
    c_raw = c_ref[...]
    xg = _conv_silu(x_raw, carx_ref[g], cwx_ref[...], cbx_ref[...])
    bg = _conv_silu(b_raw, carb_ref[g], cwb_ref[...], cbb_ref[...])
    cg = _conv_silu(c_raw, carc_ref[g], cwc_ref[...], cbc_ref[...])
    carx_ref[g] = x_raw[L - SUBLANES:, :]
    carb_ref[g] = b_raw[L - SUBLANES:, :]
    carc_ref[g] = c_raw[L - SUBLANES:, :]

    xdt = xg * dt_e
    cb16 = cg.astype(BF16)
    bb16 = bg.astype(BF16)
    cb = lax.dot_general(cb16, bb16, (((1,), (1,)), ((), ())), preferred_element_type=F32)
    li = lax.broadcasted_iota(jnp.int32, (L, L), 0)
    si = lax.broadcasted_iota(jnp.int32, (L, L), 1)
    cb = jnp.where(li >= si, cb, 0.0)

    state = state_ref[g]
    y_off = jnp.dot(cb16, state.astype(BF16), preferred_element_type=F32) * eac_e

    lane = lax.broadcasted_iota(jnp.int32, (L, LANES), 1)
    heads_per_tile = LANES // SSD_HEAD_DIM
    assert heads_per_tile == 2
    pieces = []
    for pair in range(SSD_GROUP_W // LANES):
        xp = xdt[:, pair * LANES:(pair + 1) * LANES].astype(BF16)
        ms = []
        for sub in range(heads_per_tile):
            k = pair * heads_per_tile + sub
            seg = acg[:, k:k + 1] - acg_t[k:k + 1, :]
            ms.append((cb * jnp.exp2(jnp.minimum(seg, 0.0))).astype(BF16))
        both = jnp.dot(jnp.concatenate(ms, axis=0), xp, preferred_element_type=F32)
        pieces.append(jnp.where(lane < SSD_HEAD_DIM, both[:L], both[L:]))
    y = jnp.concatenate(pieces, axis=1) + y_off + xg * dexp_ref[...]

    xtail = (xdt * tail_e).astype(BF16)
    upd = lax.dot_general(bb16, xtail, (((0,), (0,)), ((), ())), preferred_element_type=F32)
    state_ref[g] = state * eac_e[L - 1:L, :] + upd

    hz = 0.5 * z_ref[...]
    y = y * (hz + hz * jnp.tanh(hz))
    y = y * lax.rsqrt(jnp.mean(y * y, axis=-1, keepdims=True) + RMS_EPS)
    y16 = (y * ng_ref[...]).astype(BF16)
    acc_ref[...] += jnp.dot(y16, ow_ref[...], preferred_element_type=F32)

    @pl.when(g == pl.num_programs(2) - 1)
    def _():
        u = DEEPNORM_ALPHA * res_ref[...] + (1.0 + gate_ref[...]) * acc_ref[...]
        out_ref[...] = _layer_norm(u, lng_ref[...], lnb_ref[...])


def _ssd_constants():
    tril = np.tril(np.ones((SSD_CHUNK, SSD_CHUNK), np.float32))
    head_of_col = np.arange(SSD_D_INNER) // SSD_HEAD_DIM
    e = (np.arange(LANES)[:, None] == head_of_col[None, :]).astype(np.float32)
    col = np.arange(SSD_N_GROUPS * LANES)
    src = np.where(col % LANES < SSD_HEADS_PER_GROUP,
                   (col // LANES) * SSD_HEADS_PER_GROUP + col % LANES, -1)
    sel = (np.arange(LANES)[:, None] == src[None, :]).astype(np.float32)
    e3 = np.concatenate([e, e, e], axis=0)
    sel3 = np.concatenate([sel, sel, sel], axis=0)
    return jnp.asarray(tril, BF16), jnp.asarray(e3, BF16), jnp.asarray(sel3, BF16)


def _ssd_layer(zx, dt, conv_w, conv_b, dt_bias, a_log, d_skip, norm_g, out_w, x2, gate, ln_g, ln_b,
               *, batch, seq):
    t = zx.shape[0]
    d = x2.shape[1]
    L = SSD_CHUNK
    nc = seq // L
    G = SSD_N_GROUPS
    gw = SSD_GROUP_W
    ns = SSD_D_STATE
    tril, e3, sel3 = _ssd_constants()
    pad = LANES - SSD_N_HEADS
    dtb = jnp.pad(dt_bias, (0, pad)).reshape(1, LANES)
    alog = jnp.pad(a_log, (0, pad)).reshape(1, LANES)
    dexp = jnp.repeat(d_skip, SSD_HEAD_DIM).reshape(1, SSD_D_INNER)
    ng = norm_g.reshape(1, SSD_D_INNER)
    cb2 = conv_b.reshape(1, SSD_CONV_DIM)

    x_off = SSD_D_INNER // gw
    b_off = (2 * SSD_D_INNER) // ns
    c_off = (2 * SSD_D_INNER + SSD_BC_DIM) // ns
    wb_off = SSD_D_INNER // ns
    wc_off = (SSD_D_INNER + SSD_BC_DIM) // ns

    def row(b, c, g):
        return b * nc + c

    in_specs = [
        pl.BlockSpec((L, gw), lambda b, c, g: (row(b, c, g), g)),
        pl.BlockSpec((L, gw), lambda b, c, g: (row(b, c, g), x_off + g)),
        pl.BlockSpec((L, ns), lambda b, c, g: (row(b, c, g), b_off + g)),
        pl.BlockSpec((L, ns), lambda b, c, g: (row(b, c, g), c_off + g)),
        pl.BlockSpec((L, LANES), lambda b, c, g: (row(b, c, g), 0)),
        pl.BlockSpec((SSD_CONV_W, gw), lambda b, c, g: (0, g)),
        pl.BlockSpec((SSD_CONV_W, ns), lambda b, c, g: (0, wb_off + g)),
        pl.BlockSpec((SSD_CONV_W, ns), lambda b, c, g: (0, wc_off + g)),
        pl.BlockSpec((1, gw), lambda b, c, g: (0, g)),
        pl.BlockSpec((1, ns), lambda b, c, g: (0, wb_off + g)),
        pl.BlockSpec((1, ns), lambda b, c, g: (0, wc_off + g)),
        pl.BlockSpec((1, LANES), lambda b, c, g: (0, 0)),
        pl.BlockSpec((1, LANES), lambda b, c, g: (0, 0)),
        pl.BlockSpec((1, gw), lambda b, c, g: (0, g)),
        pl.BlockSpec((1, gw), lambda b, c, g: (0, g)),
        pl.BlockSpec((L, L), lambda b, c, g: (0, 0)),
        pl.BlockSpec((3 * LANES, gw), lambda b, c, g: (0, g)),
        pl.BlockSpec((3 * LANES, LANES), lambda b, c, g: (0, g)),
        pl.BlockSpec((gw, d), lambda b, c, g: (g, 0)),
        pl.BlockSpec((L, d), lambda b, c, g: (row(b, c, g), 0)),
        pl.BlockSpec((None, 1, d), lambda b, c, g: (b, 0, 0)),
        pl.BlockSpec((1, d), lambda b, c, g: (0, 0)),
        pl.BlockSpec((1, d), lambda b, c, g: (0, 0)),
    ]
    return pl.pallas_call(
        _ssd_kernel,
        grid=(batch, nc, G),
        in_specs=in_specs,
        out_specs=pl.BlockSpec((L, d), lambda b, c, g: (row(b, c, g), 0)),
        out_shape=jax.ShapeDtypeStruct((t, d), F32),
        scratch_shapes=[
            pltpu.VMEM((G, ns, gw), F32),
            pltpu.VMEM((G, SUBLANES, gw), F32),
            pltpu.VMEM((G, SUBLANES, ns), F32),
            pltpu.VMEM((G, SUBLANES, ns), F32),
            pltpu.VMEM((3 * L, 3 * LANES), BF16),
            pltpu.VMEM((L, 3 * LANES), BF16),
            pltpu.VMEM((L, d), F32),
        ],
        compiler_params=_params(("parallel", "arbitrary", "arbitrary")),
        name="ssd_mixer_ln",
    )(zx, zx, zx, zx, dt, conv_w, conv_w, conv_w, cb2, cb2, cb2, dtb, alog, dexp, ng, tril, e3, sel3,
      out_w, x2, gate, ln_g.reshape(1, d), ln_b.reshape(1, d))


def _b_proj_kernel(x_ref, scale_ref, shift_ref, w_ref, o0_ref, o1_ref, o2_ref, z_ref, h_ref, res_ref):
    j = pl.program_id(1)
    tm = x_ref.shape[0]
    n_lane_tiles = w_ref.shape[1] // LANES
    n_qkv = 3 * DIL_N_GROUPS

    @pl.when(j == 0)
    def _():
        x = x_ref[...]
        h_ref[0] = (x * (1.0 + scale_ref[...]) + shift_ref[...]).astype(BF16)
        h_ref[1] = x.astype(BF16)

    plain = ((j >= DIL_N_GROUPS) & (j < n_qkv)).astype(jnp.int32)
    acc = jnp.dot(h_ref[plain], w_ref[...], preferred_element_type=F32)
    for ct in range(n_lane_tiles):
        res_ref[ct] = acc[:, ct * LANES:(ct + 1) * LANES]

    for group, out_ref in enumerate((o0_ref, o1_ref, o2_ref)):
        dilation = DIL_PATTERNS[group][1]
        rows = tm // dilation

        @pl.when((j < n_qkv) & (j % DIL_N_GROUPS == group))
        def _(out_ref=out_ref, dilation=dilation, rows=rows):
            for r in range(dilation):
                for ct in range(n_lane_tiles):
                    out_ref[r, :, ct * LANES:(ct + 1) * LANES] = (
                        res_ref[ct, pl.ds(r, rows, stride=dilation), :].astype(BF16))

    @pl.when(j == n_qkv)
    def _():
        for ct in range(n_lane_tiles):
            z_ref[:, ct * LANES:(ct + 1) * LANES] = res_ref[ct]


def _b_proj(x2, scale, shift, w, *, batch, seq, tm):
    t, d = x2.shape
    ow = DIL_OUT_WIDTH
    per_batch = seq // tm
    n_qkv = 3 * DIL_N_GROUPS
    out_specs, out_shape = [], []
    for _, dilation in DIL_PATTERNS:
        out_specs.append(pl.BlockSpec(
            (None, dilation, tm // dilation, ow),
            lambda i, j: (i // per_batch, 0, i % per_batch, jnp.minimum(j // DIL_N_GROUPS, 2))))
        out_shape.append(jax.ShapeDtypeStruct((batch, dilation, seq // dilation, 3 * ow), BF16))
    out_specs.append(pl.BlockSpec((tm, ow), lambda i, j: (i, 0)))
    out_shape.append(jax.ShapeDtypeStruct((t, ow), F32))
    return pl.pallas_call(
        _b_proj_kernel,
        grid=(t // tm, n_qkv + 1),
        in_specs=[
            pl.BlockSpec((tm, d), lambda i, j: (i, 0)),
            pl.BlockSpec((None, 1, d), lambda i, j: (i // per_batch, 0, 0)),
            pl.BlockSpec((None, 1, d), lambda i, j: (i // per_batch, 0, 0)),
            pl.BlockSpec((d, ow), lambda i, j: (0, j)),
        ],
        out_specs=out_specs,
        out_shape=out_shape,
        scratch_shapes=[pltpu.VMEM((2, tm, d), BF16), pltpu.VMEM((ow // LANES, tm, LANES), F32)],
        compiler_params=_params(("parallel", "arbitrary")),
        name="dilated_in_proj",
    )(x2, scale, shift, w)


def _attn_bias(group):
    _, dilation = DIL_PATTERNS[group]
    n_all = DIL_N_GROUPS * DIL_HEADS
    slopes = 2.0 ** (-8.0 * np.arange(1, n_all + 1) / n_all)
    slopes = slopes.reshape(DIL_N_GROUPS, DIL_HEADS)[group].astype(np.float32)
    qi = np.arange(DIL_BLOCK)[:, None]
    kj = np.arange(2 * DIL_BLOCK)[None, :]
    delta = qi + DIL_BLOCK - kj
    valid = (delta >= 0) & (delta <= DIL_BLOCK)
    alibi = -slopes[:, None, None] * (delta * dilation).astype(np.float32)[None]
    return jnp.asarray(np.where(valid[None], alibi * np.float32(LOG2E), -np.inf).astype(np.float32))


def _attn_kernel(q_ref, k_ref, v_ref, kp_ref, vp_ref, bias_ref, o_ref, lse_ref):
    first = pl.program_id(2) == 0
    blk = DIL_BLOCK
    tq = q_ref.shape[0]
    scale = DIL_HEAD_DIM ** -0.5 * LOG2E
    dn = (((1,), (1,)), ((), ()))
    lane = lax.broadcasted_iota(jnp.int32, (blk, LANES), 1)
    kcol = lax.broadcasted_iota(jnp.int32, (blk, 2 * blk), 1)
    no_prev = first & (kcol < blk)
    for sb in range(tq // blk):
        rows = slice(sb * blk, (sb + 1) * blk)
        lse_tile = jnp.zeros((blk, LANES), F32)
        for h in range(DIL_HEADS):
            cols = slice(h * DIL_HEAD_DIM, (h + 1) * DIL_HEAD_DIM)
            q = q_ref[rows, cols]
            bias = bias_ref[h]
            if sb == 0:
                s = jnp.concatenate(
                    [lax.dot_general(q, kp_ref[:, cols], dn, preferred_element_type=F32),
                     lax.dot_general(q, k_ref[rows, cols], dn, preferred_element_type=F32)], axis=1)
                bias = jnp.where(no_prev, -jnp.inf, bias)
            else:
                keys = slice((sb - 1) * blk, (sb + 1) * blk)
                s = lax.dot_general(q, k_ref[keys, cols], dn, preferred_element_type=F32)
            s = s * scale + bias
            m = jnp.max(s, axis=-1, keepdims=True)
            p = jnp.exp2(s - m)
            den = jnp.sum(p, axis=-1, keepdims=True)
            p16 = p.astype(BF16)
            if sb == 0:
                o = (jnp.dot(p16[:, :blk], vp_ref[:, cols], preferred_element_type=F32)
                     + jnp.dot(p16[:, blk:], v_ref[rows, cols], preferred_element_type=F32))
            else:
                o = jnp.dot(p16, v_ref[keys, cols], preferred_element_type=F32)
            o_ref[rows, cols] = o * (1.0 / den)
            lse_tile = jnp.where(lane == h, (m + jnp.log2(den)) * (1.0 / LOG2E), lse_tile)
        lse_ref[rows, :] = lse_tile


def _dilated_attention(qkv, group, *, tq):
    batch, dilation, m, _ = qkv.shape
    assert DIL_PATTERNS[group][0] // dilation == DIL_BLOCK and m % tq == 0
    w = DIL_OUT_WIDTH
    blk = DIL_BLOCK
    sub = tq // blk

    def prev(i):
        return jnp.maximum(i * sub - 1, 0)

    return pl.pallas_call(
        _attn_kernel,
        grid=(batch, dilation, m // tq),
        in_specs=[
            pl.BlockSpec((None, None, tq, w), lambda b, r, i: (b, r, i, 0)),
            pl.BlockSpec((None, None, tq, w), lambda b, r, i: (b, r, i, 1)),
            pl.BlockSpec((None, None, tq, w), lambda b, r, i: (b, r, i, 2)),
            pl.BlockSpec((None, None, blk, w), lambda b, r, i: (b, r, prev(i), 1)),
            pl.BlockSpec((None, None, blk, w), lambda b, r, i: (b, r, prev(i), 2)),
            pl.BlockSpec((DIL_HEADS, blk, 2 * blk), lambda b, r, i: (0, 0, 0)),
        ],
        out_specs=[
            pl.BlockSpec((None, None, tq, w), lambda b, r, i: (b, r, i, 0)),
            pl.BlockSpec((None, None, tq, LANES), lambda b, r, i: (b, r, i, 0)),
        ],
        out_shape=[
            jax.ShapeDtypeStruct((batch, dilation, m, w), F32),
            jax.ShapeDtypeStruct((batch, dilation, m, LANES), F32),
        ],
        compiler_params=_params(("parallel", "parallel", "arbitrary")),
        name=f"dilated_attn_{group}",
    )(qkv, qkv, qkv, qkv, qkv, _attn_bias(group))


def _merge_ln_kernel(o0_ref, o1_ref, o2_ref, l0_ref, l1_ref, l2_ref, z_ref, w_ref, x_ref,
                     gate_ref, g_ref, b_ref, out_ref, on_ref, ln_ref):
    tm = x_ref.shape[0]
    assert DIL_PATTERNS[0][1] == 1
    for g, (o_ref, l_ref) in ((1, (o1_ref, l1_ref)), (2, (o2_ref, l2_ref))):
        dilation = DIL_PATTERNS[g][1]
        rows = tm // dilation
        for r in range(dilation):
            ln_ref[g - 1, pl.ds(r, rows, stride=dilation), :] = l_ref[r]
            for h in range(DIL_HEADS):
                on_ref[g - 1, h, pl.ds(r, rows, stride=dilation), :] = (
                    o_ref[r, :, h * DIL_HEAD_DIM:(h + 1) * DIL_HEAD_DIM])
    l0, l1, l2 = l0_ref[0], ln_ref[0], ln_ref[1]
    mx = jnp.maximum(jnp.maximum(l0, l1), l2)
    e0, e1, e2 = jnp.exp(l0 - mx), jnp.exp(l1 - mx), jnp.exp(l2 - mx)
    den = e0 + e1 + e2
    w0, w1, w2 = e0 / den, e1 / den, e2 / den
    pieces = []
    for h in range(DIL_HEADS):
        pieces.append(o0_ref[0, :, h * DIL_HEAD_DIM:(h + 1) * DIL_HEAD_DIM] * w0[:, h:h + 1]
                      + on_ref[0, h] * w1[:, h:h + 1]
                      + on_ref[1, h] * w2[:, h:h + 1])
    o = jnp.concatenate(pieces, axis=1) * _silu(z_ref[...])
    y = jnp.dot(o.astype(BF16), w_ref[...], preferred_element_type=F32)
    u = DEEPNORM_ALPHA * x_ref[...] + (1.0 + gate_ref[...]) * y
    out_ref[...] = _layer_norm(u, g_ref[...], b_ref[...])


def _merge_ln(os_, lses, z, w, x2, gate, ln_g, ln_b, *, seq, tm):
    t, d = x2.shape
    ow = DIL_OUT_WIDTH
    per_batch = seq // tm
    row = lambda i: (i, 0)

    def sub_major(width):
        return [pl.BlockSpec((None, dilation, tm // dilation, width),
                             lambda i: (i // per_batch, 0, i % per_batch, 0))
                for _, dilation in DIL_PATTERNS]

    return pl.pallas_call(
        _merge_ln_kernel,
        grid=(t // tm,),
        in_specs=sub_major(ow) + sub_major(LANES) + [
            pl.BlockSpec((tm, ow), row),
            pl.BlockSpec((ow, d), lambda i: (0, 0)),
            pl.BlockSpec((tm, d), row),
            pl.BlockSpec((None, 1, d), lambda i: (i // per_batch, 0, 0)),
            pl.BlockSpec((1, d), lambda i: (0, 0)),
            pl.BlockSpec((1, d), lambda i: (0, 0)),
        ],
        out_specs=pl.BlockSpec((tm, d), row),
        out_shape=jax.ShapeDtypeStruct((t, d), F32),
        scratch_shapes=[pltpu.VMEM((DIL_N_GROUPS - 1, DIL_HEADS, tm, LANES), F32),
                        pltpu.VMEM((DIL_N_GROUPS - 1, tm, LANES), F32)],
        compiler_params=_params(("parallel",)),
        name="merge_out_proj_ln",
    )(*os_, *lses, z, w, x2, gate, ln_g.reshape(1, d), ln_b.reshape(1, d))


def kernel(x, c, ada_w, ada_b, ln_g, ln_b, a_in_w, a_conv_w, a_conv_b, a_dt_bias, a_A_log, a_D,
           a_norm_g, a_out_w, kv_w, b_in_w, b_out_w):
    batch, seq, d = x.shape
    t = batch * seq
    x2 = x.reshape(t, d)

    c_pad = jnp.pad(c, ((0, 2 * SUBLANES - batch), (0, 0)))
    mod = _adaln(c_pad, ada_w, ada_b)[:, :batch]
    shift = mod[:, :, None, 0:d]
    scale = mod[:, :, None, d:2 * d]
    gate = mod[:, :, None, 2 * d:3 * d]

    zx_w = a_in_w[0, :, :SSD_D_INNER + SSD_CONV_DIM].astype(BF16)
    dt_w = jnp.pad(a_in_w[0, :, SSD_D_INNER + SSD_CONV_DIM:], ((0, 0), (0, LANES - SSD_N_HEADS))).astype(BF16)
    zx, dt = _mod_matmul(x2, scale[0], shift[0], zx_w, F32, seq=seq, modulate=True,
                         tm=1024, tn=1024, wdt=dt_w, name="ssd_in_proj")
    x1 = _ssd_layer(zx, dt, a_conv_w[0], a_conv_b[0], a_dt_bias[0], a_A_log[0], a_D[0], a_norm_g[0],
                    a_out_w[0].astype(BF16), x2, gate[0], ln_g[0], ln_b[0], batch=batch, seq=seq)

    b_w = jnp.concatenate([b_in_w[0, :, :DIL_Q_WIDTH], kv_w, b_in_w[0, :, DIL_Q_WIDTH:]], axis=1).astype(BF16)
    *qkvs, z = _b_proj(x1, scale[1], shift[1], b_w, batch=batch, seq=seq, tm=512)

    os_, lses = [], []
    for group in range(DIL_N_GROUPS):
        o, lse = _dilated_attention(qkvs[group], group, tq=min(512, seq // DIL_PATTERNS[group][1]))
        os_.append(o)
        lses.append(lse)

    out = _merge_ln(os_, lses, z, b_out_w[0].astype(BF16), x1, gate[1], ln_g[1], ln_b[1], seq=seq, tm=256)
    return out.reshape(batch, seq, d)
```

```python
import numpy as np
import jax
import jax.numpy as jnp
from jax import lax
from jax.experimental import pallas as pl
from jax.experimental.pallas import tpu as pltpu

F32 = jnp.float32
BF16 = jnp.bfloat16

D_MODEL = 2048
DEPTH = 2
LANES = 128
SUBLANES = 8

SSD_D_INNER = 2 * D_MODEL
SSD_HEAD_DIM = 64
SSD_N_HEADS = SSD_D_INNER // SSD_HEAD_DIM
SSD_N_GROUPS = 8
SSD_HEADS_PER_GROUP = SSD_N_HEADS // SSD_N_GROUPS
SSD_D_STATE = 128
SSD_CONV_W = 4
SSD_CHUNK = 256
SSD_BC_DIM = SSD_N_GROUPS * SSD_D_STATE
SSD_CONV_DIM = SSD_D_INNER + 2 * SSD_BC_DIM
SSD_GROUP_W = SSD_HEADS_PER_GROUP * SSD_HEAD_DIM

DIL_PATTERNS = ((128, 1), (512, 4), (2048, 16))
DIL_N_GROUPS = len(DIL_PATTERNS)
DIL_HEADS = 8
DIL_HEAD_DIM = 128
DIL_Q_WIDTH = DIL_N_GROUPS * DIL_HEADS * DIL_HEAD_DIM
DIL_OUT_WIDTH = DIL_HEADS * DIL_HEAD_DIM
DIL_BLOCK = 128
ATTN_ROWS = 512

DEEPNORM_ALPHA = (2 * DEPTH) ** 0.25
LN_EPS = 1e-5
RMS_EPS = 1e-5
LOG2E = 1.4426950408889634

VMEM_LIMIT = 56 * 1024 * 1024


def _params(sem):
    return pltpu.CompilerParams(dimension_semantics=sem, vmem_limit_bytes=VMEM_LIMIT)


def _silu(v):
    return v * (1.0 / (1.0 + jnp.exp(-v)))


def _split3(v):
    v1 = v.astype(BF16)
    r1 = v - v1.astype(F32)
    v2 = r1.astype(BF16)
    r2 = r1 - v2.astype(F32)
    v3 = r2.astype(BF16)
    return jnp.concatenate([v1, v2, v3], axis=1)


def _layer_norm(u, g, b):
    mu = jnp.mean(u, axis=-1, keepdims=True)
    d = u - mu
    var = jnp.mean(d * d, axis=-1, keepdims=True)
    return d * lax.rsqrt(var + LN_EPS) * g + b


def _adaln_kernel(c_ref, w_ref, b_ref, o_ref):
    s = _silu(c_ref[...]).astype(BF16)
    o_ref[...] = jnp.dot(s, w_ref[...].astype(BF16), preferred_element_type=F32) + b_ref[...]


def _adaln(c_pad, ada_w, ada_b):
    rows = c_pad.shape[0]
    n = 3 * D_MODEL
    tn = 768
    return pl.pallas_call(
        _adaln_kernel,
        grid=(DEPTH, n // tn),
        in_specs=[
            pl.BlockSpec((rows, D_MODEL), lambda l, j: (0, 0)),
            pl.BlockSpec((None, D_MODEL, tn), lambda l, j: (l, 0, j)),
            pl.BlockSpec((None, 1, tn), lambda l, j: (l, 0, j)),
        ],
        out_specs=pl.BlockSpec((None, rows, tn), lambda l, j: (l, 0, j)),
        out_shape=jax.ShapeDtypeStruct((DEPTH, rows, n), F32),
        compiler_params=_params(("parallel", "parallel")),
        name="adaln",
    )(c_pad, ada_w, ada_b.reshape(DEPTH, 1, n))


def _in_proj_kernel(x_ref, scale_ref, shift_ref, w_ref, wdt_ref, o_ref, dt_ref, h_ref):
    @pl.when(pl.program_id(1) == 0)
    def _():
        h = (x_ref[...] * (1.0 + scale_ref[...]) + shift_ref[...]).astype(BF16)
        h_ref[...] = h
        dt_ref[...] = jnp.dot(h, wdt_ref[...], preferred_element_type=F32)

    o_ref[...] = jnp.dot(h_ref[...], w_ref[...], preferred_element_type=F32)


def _in_proj(x2, scale, shift, w, wdt, *, seq, tm, tn):
    t, d = x2.shape
    n = w.shape[1]
    per_batch = seq // tm
    return pl.pallas_call(
        _in_proj_kernel,
        grid=(t // tm, n // tn),
        in_specs=[
            pl.BlockSpec((tm, d), lambda i, j: (i, 0)),
            pl.BlockSpec((None, 1, d), lambda i, j: (i // per_batch, 0, 0)),
            pl.BlockSpec((None, 1, d), lambda i, j: (i // per_batch, 0, 0)),
            pl.BlockSpec((d, tn), lambda i, j: (0, j)),
            pl.BlockSpec((d, LANES), lambda i, j: (0, 0)),
        ],
        out_specs=[pl.BlockSpec((tm, tn), lambda i, j: (i, j)), pl.BlockSpec((tm, LANES), lambda i, j: (i, 0))],
        out_shape=[jax.ShapeDtypeStruct((t, n), F32), jax.ShapeDtypeStruct((t, LANES), F32)],
        scratch_shapes=[pltpu.VMEM((tm, d), BF16)],
        compiler_params=_params(("parallel", "arbitrary")),
        name="ssd_in_proj",
    )(x2, scale, shift, w, wdt)


def _conv_silu(raw, carry, w, b):
    wh = 0.5 * w
    acc = raw * wh[SSD_CONV_W - 1:SSD_CONV_W, :] + 0.5 * b
    row = lax.broadcasted_iota(jnp.int32, carry.shape, 0)
    for s in range(1, SSD_CONV_W):
        sh = pltpu.roll(raw, s, axis=0)
        prev = pltpu.roll(carry, s, axis=0)
        head = jnp.where(row < s, prev, sh[:SUBLANES])
        sh = jnp.concatenate([head, sh[SUBLANES:]], axis=0)
        acc = acc + sh * wh[SSD_CONV_W - 1 - s:SSD_CONV_W - s, :]
    return acc + acc * jnp.tanh(acc)


def _ssd_kernel(z_ref, x_ref, b_ref, c_ref, dt_ref,
                cwx_ref, cwb_ref, cwc_ref, cbx_ref, cbb_ref, cbc_ref,
                dtb_ref, alog_ref, dexp_ref, ng_ref, tril_ref, e_ref, sel_ref,
                y_ref,
                state_ref, carx_ref, carb_ref, carc_ref, v3_ref, ac3_ref):
    c = pl.program_id(1)
    g = pl.program_id(2)
    L = SSD_CHUNK

    @pl.when(c == 0)
    def _():
        state_ref[g] = jnp.zeros(state_ref.shape[1:], F32)
        carx_ref[g] = jnp.zeros(carx_ref.shape[1:], F32)
        carb_ref[g] = jnp.zeros(carb_ref.shape[1:], F32)
        carc_ref[g] = jnp.zeros(carc_ref.shape[1:], F32)

    @pl.when(g == 0)
    def _():
        v = dt_ref[...] + dtb_ref[...]
        dt = jnp.maximum(v, 0.0) + jnp.log1p(jnp.exp(-jnp.abs(v)))
        a = dt * (-jnp.exp(alog_ref[...]))
        a3 = _split3(a)
        tril = tril_ref[...]
        acum = (jnp.dot(tril, a3[:, :LANES], preferred_element_type=F32)
                + jnp.dot(tril, a3[:, LANES:2 * LANES], preferred_element_type=F32)
                + jnp.dot(tril, a3[:, 2 * LANES:], preferred_element_type=F32))
        tail = jnp.exp(acum[L - 1:L, :] - acum)
        v3_ref[0:L, :] = _split3(dt)
        v3_ref[L:2 * L, :] = _split3(tail)
        v3_ref[2 * L:3 * L, :] = _split3(jnp.exp(acum))
        ac3_ref[...] = _split3(acum)

    ex = jnp.dot(v3_ref[...], e_ref[...], preferred_element_type=F32)
    dt_e = ex[0:L]
    tail_e = ex[L:2 * L]
    eac_e = ex[2 * L:3 * L]
    acg = jnp.dot(ac3_ref[...], sel_ref[...], preferred_element_type=F32) * LOG2E
    acg_t = acg.T

    x_raw = x_ref[...]
    b_raw = b_ref[...]
    c_raw = c_ref[...]
    xg = _conv_silu(x_raw, carx_ref[g], cwx_ref[...], cbx_ref[...])
    bg = _conv_silu(b_raw, carb_ref[g], cwb_ref[...], cbb_ref[...])
    cg = _conv_silu(c_raw, carc_ref[g], cwc_ref[...], cbc_ref[...])
    carx_ref[g] = x_raw[L - SUBLANES:, :]
    carb_ref[g] = b_raw[L - SUBLANES:, :]
    carc_ref[g] = c_raw[L - SUBLANES:, :]

    xdt = xg * dt_e
    cb16 = cg.astype(BF16)
    bb16 = bg.astype(BF16)
    cb = lax.dot_general(cb16, bb16, (((1,), (1,)), ((), ())), preferred_element_type=F32)
    li = lax.broadcasted_iota(jnp.int32, (L, L), 0)
    si = lax.broadcasted_iota(jnp.int32, (L, L), 1)
    cb = jnp.where(li >= si, cb, 0.0)

    state = state_ref[g]
    y_off = jnp.dot(cb16, state.astype(BF16), preferred_element_type=F32) * eac_e

    lane = lax.broadcasted_iota(jnp.int32, (L, LANES), 1)
    heads_per_tile = LANES // SSD_HEAD_DIM
    assert heads_per_tile == 2
    n_pairs = SSD_GROUP_W // LANES
    stacked = []
    for pair in range(n_pairs):
        ms = []
        for sub in range(heads_per_tile):
            k = pair * heads_per_tile + sub
            seg = acg[:, k:k + 1] - acg_t[k:k + 1, :]
            ms.append((cb * jnp.exp2(jnp.minimum(seg, 0.0))).astype(BF16))
        stacked.append(jnp.concatenate(ms, axis=0))
    pieces = []
    for pair in range(n_pairs):
        xp = xdt[:, pair * LANES:(pair + 1) * LANES].astype(BF16)
        both = jnp.dot(stacked[pair], xp, preferred_element_type=F32)
        pieces.append(jnp.where(lane < SSD_HEAD_DIM, both[:L], both[L:]))
    y = jnp.concatenate(pieces, axis=1) + y_off + xg * dexp_ref[...]

    xtail = (xdt * tail_e).astype(BF16)
    upd = lax.dot_general(bb16, xtail, (((0,), (0,)), ((), ())), preferred_element_type=F32)
    state_ref[g] = state * eac_e[L - 1:L, :] + upd

    hz = 0.5 * z_ref[...]
    y = y * (hz + hz * jnp.tanh(hz))
    y = y * lax.rsqrt(jnp.mean(y * y, axis=-1, keepdims=True) + RMS_EPS)
    y_ref[...] = (y * ng_ref[...]).astype(y_ref.dtype)


def _ssd_constants():
    tril = np.tril(np.ones((SSD_CHUNK, SSD_CHUNK), np.float32))
    head_of_col = np.arange(SSD_D_INNER) // SSD_HEAD_DIM
    e = (np.arange(LANES)[:, None] == head_of_col[None, :]).astype(np.float32)
    col = np.arange(SSD_N_GROUPS * LANES)
    src = np.where(col % LANES < SSD_HEADS_PER_GROUP,
                   (col // LANES) * SSD_HEADS_PER_GROUP + col % LANES, -1)
    sel = (np.arange(LANES)[:, None] == src[None, :]).astype(np.float32)
    e3 = np.concatenate([e, e, e], axis=0)
    sel3 = np.concatenate([sel, sel, sel], axis=0)
    return jnp.asarray(tril, BF16), jnp.asarray(e3, BF16), jnp.asarray(sel3, BF16)


def _ssd_scan(zx, dt, conv_w, conv_b, dt_bias, a_log, d_skip, norm_g, *, batch, seq):
    t = zx.shape[0]
    L = SSD_CHUNK
    nc = seq // L
    G = SSD_N_GROUPS
    gw = SSD_GROUP_W
    ns = SSD_D_STATE
    tril, e3, sel3 = _ssd_constants()
    pad = LANES - SSD_N_HEADS
    dtb = jnp.pad(dt_bias, (0, pad)).reshape(1, LANES)
    alog = jnp.pad(a_log, (0, pad)).reshape(1, LANES)
    dexp = jnp.repeat(d_skip, SSD_HEAD_DIM).reshape(1, SSD_D_INNER)
    ng = norm_g.reshape(1, SSD_D_INNER)
    cb2 = conv_b.reshape(1, SSD_CONV_DIM)

    x_off = SSD_D_INNER // gw
    b_off = (2 * SSD_D_INNER) // ns
    c_off = (2 * SSD_D_INNER + SSD_BC_DIM) // ns
    wb_off = SSD_D_INNER // ns
    wc_off = (SSD_D_INNER + SSD_BC_DIM) // ns

    def row(b, c, g):
        return b * nc + c

    in_specs = [
        pl.BlockSpec((L, gw), lambda b, c, g: (row(b, c, g), g)),
        pl.BlockSpec((L, gw), lambda b, c, g: (row(b, c, g), x_off + g)),
        pl.BlockSpec((L, ns), lambda b, c, g: (row(b, c, g), b_off + g)),
        pl.BlockSpec((L, ns), lambda b, c, g: (row(b, c, g), c_off + g)),
        pl.BlockSpec((L, LANES), lambda b, c, g: (row(b, c, g), 0)),
        pl.BlockSpec((SSD_CONV_W, gw), lambda b, c, g: (0, g)),
        pl.BlockSpec((SSD_CONV_W, ns), lambda b, c, g: (0, wb_off + g)),
        pl.BlockSpec((SSD_CONV_W, ns), lambda b, c, g: (0, wc_off + g)),
        pl.BlockSpec((1, gw), lambda b, c, g: (0, g)),
        pl.BlockSpec((1, ns), lambda b, c, g: (0, wb_off + g)),
        pl.BlockSpec((1, ns), lambda b, c, g: (0, wc_off + g)),
        pl.BlockSpec((1, LANES), lambda b, c, g: (0, 0)),
        pl.BlockSpec((1, LANES), lambda b, c, g: (0, 0)),
        pl.BlockSpec((1, gw), lambda b, c, g: (0, g)),
        pl.BlockSpec((1, gw), lambda b, c, g: (0, g)),
        pl.BlockSpec((L, L), lambda b, c, g: (0, 0)),
        pl.BlockSpec((3 * LANES, gw), lambda b, c, g: (0, g)),
        pl.BlockSpec((3 * LANES, LANES), lambda b, c, g: (0, g)),
    ]
    return pl.pallas_call(
        _ssd_kernel,
        grid=(batch, nc, G),
        in_specs=in_specs,
        out_specs=pl.BlockSpec((L, gw), lambda b, c, g: (row(b, c, g), g)),
        out_shape=jax.ShapeDtypeStruct((t, SSD_D_INNER), BF16),
        scratch_shapes=[
            pltpu.VMEM((G, ns, gw), F32),
            pltpu.VMEM((G, SUBLANES, gw), F32),
            pltpu.VMEM((G, SUBLANES, ns), F32),
            pltpu.VMEM((G, SUBLANES, ns), F32),
            pltpu.VMEM((3 * L, 3 * LANES), BF16),
            pltpu.VMEM((L, 3 * LANES), BF16),
        ],
        compiler_params=_params(("parallel", "arbitrary", "arbitrary")),
        name="ssd_scan",
    )(zx, zx, zx, zx, dt, conv_w, conv_w, conv_w, cb2, cb2, cb2, dtb, alog, dexp, ng, tril, e3, sel3)


def _proj_ln_kernel(y_ref, w_ref, x_ref, gate_ref, g_ref, b_ref, o_ref):
    acc = jnp.dot(y_ref[...], w_ref[...], preferred_element_type=F32)
    u = DEEPNORM_ALPHA * x_ref[...] + (1.0 + gate_ref[...]) * acc
    o_ref[...] = _layer_norm(u, g_ref[...], b_ref[...])


def _proj_ln(y, w, x2, gate, ln_g, ln_b, *, seq, tm):
    t, kdim = y.shape
    d = w.shape[1]
    per_batch = seq // tm
    return pl.pallas_call(
        _proj_ln_kernel,
        grid=(t // tm,),
        in_specs=[
            pl.BlockSpec((tm, kdim), lambda i: (i, 0)),
            pl.BlockSpec((kdim, d), lambda i: (0, 0), pipeline_mode=pl.Buffered(1)),
            pl.BlockSpec((tm, d), lambda i: (i, 0)),
            pl.BlockSpec((None, 1, d), lambda i: (i // per_batch, 0, 0)),
            pl.BlockSpec((1, d), lambda i: (0, 0)),
            pl.BlockSpec((1, d), lambda i: (0, 0)),
        ],
        out_specs=pl.BlockSpec((tm, d), lambda i: (i, 0)),
        out_shape=jax.ShapeDtypeStruct((t, d), F32),
        compiler_params=_params(("parallel",)),
        name="out_proj_ln",
    )(y, w, x2, gate, ln_g.reshape(1, d), ln_b.reshape(1, d))


def _b_proj_kernel(x_ref, scale_ref, shift_ref, w_ref, o0_ref, o1_ref, o2_ref, z_ref, h_ref, res_ref):
    j = pl.program_id(1)
    tm = x_ref.shape[0]
    n_lane_tiles = w_ref.shape[1] // LANES
    n_qkv = 3 * DIL_N_GROUPS

    @pl.when(j == 0)
    def _():
        x = x_ref[...]
        h_ref[0] = (x * (1.0 + scale_ref[...]) + shift_ref[...]).astype(BF16)
        h_ref[1] = x.astype(BF16)

    plain = ((j >= DIL_N_GROUPS) & (j < n_qkv)).astype(jnp.int32)
    acc = jnp.dot(h_ref[plain], w_ref[...], preferred_element_type=F32)
    for ct in range(n_lane_tiles):
        res_ref[ct] = acc[:, ct * LANES:(ct + 1) * LANES]

    for group, out_ref in enumerate((o0_ref, o1_ref, o2_ref)):
        dilation = DIL_PATTERNS[group][1]
        rows = tm // dilation

        @pl.when((j < n_qkv) & (j % DIL_N_GROUPS == group))
        def _(out_ref=out_ref, dilation=dilation, rows=rows):
            for r in range(dilation):
                for ct in range(n_lane_tiles):
                    out_ref[r, :, ct * LANES:(ct + 1) * LANES] = (
                        res_ref[ct, pl.ds(r, rows, stride=dilation), :].astype(BF16))

    @pl.when(j == n_qkv)
    def _():
        for ct in range(n_lane_tiles):
            z_ref[:, ct * LANES:(ct + 1) * LANES] = res_ref[ct]


def _b_proj(x2, scale, shift, w, *, batch, seq, tm):
    t, d = x2.shape
    ow = DIL_OUT_WIDTH
    per_batch = seq // tm
    n_qkv = 3 * DIL_N_GROUPS
    out_specs, out_shape = [], []
    for _, dilation in DIL_PATTERNS:
        out_specs.append(pl.BlockSpec(
            (None, dilation, tm // dilation, ow),
            lambda i, j: (i // per_batch, 0, i % per_batch, jnp.minimum(j // DIL_N_GROUPS, 2))))
        out_shape.append(jax.ShapeDtypeStruct((batch, dilation, seq // dilation, 3 * ow), BF16))
    out_specs.append(pl.BlockSpec((tm, ow), lambda i, j: (i, 0)))
    out_shape.append(jax.ShapeDtypeStruct((t, ow), F32))
    return pl.pallas_call(
        _b_proj_kernel,
        grid=(t // tm, n_qkv + 1),
        in_specs=[
            pl.BlockSpec((tm, d), lambda i, j: (i, 0)),
            pl.BlockSpec((None, 1, d), lambda i, j: (i // per_batch, 0, 0)),
            pl.BlockSpec((None, 1, d), lambda i, j: (i // per_batch, 0, 0)),
            pl.BlockSpec((d, ow), lambda i, j: (0, j)),
        ],
        out_specs=out_specs,
        out_shape=out_shape,
        scratch_shapes=[pltpu.VMEM((2, tm, d), BF16), pltpu.VMEM((ow // LANES, tm, LANES), F32)],
        compiler_params=_params(("parallel", "arbitrary")),
        name="dilated_in_proj",
    )(x2, scale, shift, w)


def _attn_bias(group):
    _, dilation = DIL_PATTERNS[group]
    n_all = DIL_N_GROUPS * DIL_HEADS
    slopes = 2.0 ** (-8.0 * np.arange(1, n_all + 1) / n_all)
    slopes = slopes.reshape(DIL_N_GROUPS, DIL_HEADS)[group].astype(np.float32)
    qi = np.arange(DIL_BLOCK)[:, None]
    kj = np.arange(2 * DIL_BLOCK)[None, :]
    delta = qi + DIL_BLOCK - kj
    valid = (delta >= 0) & (delta <= DIL_BLOCK)
    alibi = -slopes[:, None, None] * (delta * dilation).astype(np.float32)[None]
    return jnp.asarray(np.where(valid[None], alibi * np.float32(LOG2E), -np.inf).astype(np.float32))


def _attn_kernel(q_ref, k_ref, v_ref, kp_ref, vp_ref, bias_ref, o_ref, st_ref):
    first = pl.program_id(2) == 0
    blk = DIL_BLOCK
    n_sub, tq = q_ref.shape[0], q_ref.shape[1]
    scale = DIL_HEAD_DIM ** -0.5 * LOG2E
    dn = (((1,), (1,)), ((), ()))
    lane = lax.broadcasted_iota(jnp.int32, (blk, LANES), 1)
    kcol = lax.broadcasted_iota(jnp.int32, (blk, 2 * blk), 1)
    no_prev = first & (kcol < blk)
    for r in range(n_sub):
        for sb in range(tq // blk):
            rows = slice(sb * blk, (sb + 1) * blk)
            keys = slice((sb - 1) * blk, (sb + 1) * blk)
            m_tile = jnp.zeros((blk, LANES), F32)
            den_tile = jnp.ones((blk, LANES), F32)
            ss = []
            for h in range(DIL_HEADS):
                cols = slice(h * DIL_HEAD_DIM, (h + 1) * DIL_HEAD_DIM)
                q = q_ref[r, rows, cols]
                if sb == 0:
                    s = jnp.concatenate(
                        [lax.dot_general(q, kp_ref[r, :, cols], dn, preferred_element_type=F32),
                         lax.dot_general(q, k_ref[r, rows, cols], dn, preferred_element_type=F32)], axis=1)
                else:
                    s = lax.dot_general(q, k_ref[r, keys, cols], dn, preferred_element_type=F32)
                ss.append(s)
            ps = []
            for h in range(DIL_HEADS):
                bias = bias_ref[h]
                if sb == 0:
                    bias = jnp.where(no_prev, -jnp.inf, bias)
                s = ss[h] * scale + bias
                m = jnp.max(s, axis=-1, keepdims=True)
                p = jnp.exp2(s - m)
                den = jnp.sum(p, axis=-1, keepdims=True)
                ps.append(p.astype(BF16))
                m_tile = jnp.where(lane == h, m, m_tile)
                den_tile = jnp.where(lane == h, den, den_tile)
            for h in range(DIL_HEADS):
                cols = slice(h * DIL_HEAD_DIM, (h + 1) * DIL_HEAD_DIM)
                p16 = ps[h]
                if sb == 0:
                    o = (jnp.dot(p16[:, :blk], vp_ref[r, :, cols], preferred_element_type=F32)
                         + jnp.dot(p16[:, blk:], v_ref[r, rows, cols], preferred_element_type=F32))
                else:
                    o = jnp.dot(p16, v_ref[r, keys, cols], preferred_element_type=F32)
                o_ref[r, rows, cols] = o
            st_ref[r, rows, :LANES] = (m_tile + jnp.log2(den_tile)) * (1.0 / LOG2E)
            st_ref[r, rows, LANES:] = 1.0 / den_tile


def _dilated_attention(qkv, group, *, n_sub, tq):
    batch, dilation, m, _ = qkv.shape
    assert DIL_PATTERNS[group][0] // dilation == DIL_BLOCK and m % tq == 0 and dilation % n_sub == 0
    w = DIL_OUT_WIDTH
    blk = DIL_BLOCK
    sub = tq // blk

    def prev(i):
        return jnp.maximum(i * sub - 1, 0)

    return pl.pallas_call(
        _attn_kernel,
        grid=(batch, dilation // n_sub, m // tq),
        in_specs=[
            pl.BlockSpec((None, n_sub, tq, w), lambda b, r, i: (b, r, i, 0)),
            pl.BlockSpec((None, n_sub, tq, w), lambda b, r, i: (b, r, i, 1)),
            pl.BlockSpec((None, n_sub, tq, w), lambda b, r, i: (b, r, i, 2)),
            pl.BlockSpec((None, n_sub, blk, w), lambda b, r, i: (b, r, prev(i), 1)),
            pl.BlockSpec((None, n_sub, blk, w), lambda b, r, i: (b, r, prev(i), 2)),
            pl.BlockSpec((DIL_HEADS, blk, 2 * blk), lambda b, r, i: (0, 0, 0)),
        ],
        out_specs=[
            pl.BlockSpec((None, n_sub, tq, w), lambda b, r, i: (b, r, i, 0)),
            pl.BlockSpec((None, n_sub, tq, 2 * LANES), lambda b, r, i: (b, r, i, 0)),
        ],
        out_shape=[
            jax.ShapeDtypeStruct((batch, dilation, m, w), F32),
            jax.ShapeDtypeStruct((batch, dilation, m, 2 * LANES), F32),
        ],
        compiler_params=_params(("parallel", "parallel", "arbitrary")),
        name=f"dilated_attn_{group}",
    )(qkv, qkv, qkv, qkv, qkv, _attn_bias(group))


def _merge_ln_kernel(o0_ref, o1_ref, o2_ref, s0_ref, s1_ref, s2_ref, z_ref, w_ref, x_ref,
                     gate_ref, g_ref, b_ref, out_ref, on_ref, sn_ref):
    tm = x_ref.shape[0]
    assert DIL_PATTERNS[0][1] == 1
    for g, (o_ref, s_ref) in ((1, (o1_ref, s1_ref)), (2, (o2_ref, s2_ref))):
        dilation = DIL_PATTERNS[g][1]
        rows = tm // dilation
        for r in range(dilation):
            for half in range(2):
                sn_ref[g - 1, half, pl.ds(r, rows, stride=dilation), :] = s_ref[r, :, half * LANES:(half + 1) * LANES]
            for h in range(DIL_HEADS):
                on_ref[g - 1, h, pl.ds(r, rows, stride=dilation), :] = (
                    o_ref[r, :, h * DIL_HEAD_DIM:(h + 1) * DIL_HEAD_DIM])
    l0, l1, l2 = s0_ref[0, :, :LANES], sn_ref[0, 0], sn_ref[1, 0]
    mx = jnp.maximum(jnp.maximum(l0, l1), l2)
    e0, e1, e2 = jnp.exp(l0 - mx), jnp.exp(l1 - mx), jnp.exp(l2 - mx)
    inv = 1.0 / (e0 + e1 + e2)
    c0 = e0 * inv * s0_ref[0, :, LANES:]
    c1 = e1 * inv * sn_ref[0, 1]
    c2 = e2 * inv * sn_ref[1, 1]
    pieces = []
    for h in range(DIL_HEADS):
        pieces.append(o0_ref[0, :, h * DIL_HEAD_DIM:(h + 1) * DIL_HEAD_DIM] * c0[:, h:h + 1]
                      + on_ref[0, h] * c1[:, h:h + 1]
                      + on_ref[1, h] * c2[:, h:h + 1])
    hz = 0.5 * z_ref[...]
    o = jnp.concatenate(pieces, axis=1) * (hz + hz * jnp.tanh(hz))
    y = jnp.dot(o.astype(BF16), w_ref[...], preferred_element_type=F32)
    u = DEEPNORM_ALPHA * x_ref[...] + (1.0 + gate_ref[...]) * y
    out_ref[...] = _layer_norm(u, g_ref[...], b_ref[...])


def _merge_ln(os_, stats, z, w, x2, gate, ln_g, ln_b, *, seq, tm):
    t, d = x2.shape
    ow = DIL_OUT_WIDTH
    per_batch = seq // tm
    row = lambda i: (i, 0)

    def sub_major(width):
        return [pl.BlockSpec((None, dilation, tm // dilation, width),
                             lambda i: (i // per_batch, 0, i % per_batch, 0))
                for _, dilation in DIL_PATTERNS]

    return pl.pallas_call(
        _merge_ln_kernel,
        grid=(t // tm,),
        in_specs=sub_major(ow) + sub_major(2 * LANES) + [
            pl.BlockSpec((tm, ow), row),
            pl.BlockSpec((ow, d), lambda i: (0, 0)),
            pl.BlockSpec((tm, d), row),
            pl.BlockSpec((None, 1, d), lambda i: (i // per_batch, 0, 0)),
            pl.BlockSpec((1, d), lambda i: (0, 0)),
            pl.BlockSpec((1, d), lambda i: (0, 0)),
        ],
        out_specs=pl.BlockSpec((tm, d), row),
        out_shape=jax.ShapeDtypeStruct((t, d), F32),
        scratch_shapes=[pltpu.VMEM((DIL_N_GROUPS - 1, DIL_HEADS, tm, LANES), F32),
                        pltpu.VMEM((DIL_N_GROUPS - 1, 2, tm, LANES), F32)],
        compiler_params=_params(("parallel",)),
        name="merge_out_proj_ln",
    )(*os_, *stats, z, w, x2, gate, ln_g.reshape(1, d), ln_b.reshape(1, d))


def _ssd_block(x2, scale, shift, gate, in_w, conv_w, conv_b, dt_bias, a_log, d_skip, norm_g, out_w, ln_g, ln_b,
               *, batch, seq):
    zx_w = in_w[:, :SSD_D_INNER + SSD_CONV_DIM].astype(BF16)
    dt_w = jnp.pad(in_w[:, SSD_D_INNER + SSD_CONV_DIM:], ((0, 0), (0, LANES - SSD_N_HEADS))).astype(BF16)
    zx, dt = _in_proj(x2, scale, shift, zx_w, dt_w, seq=seq, tm=1024, tn=1024)
    y = _ssd_scan(zx, dt, conv_w, conv_b, dt_bias, a_log, d_skip, norm_g, batch=batch, seq=seq)
    return _proj_ln(y, out_w.astype(BF16), x2, gate, ln_g, ln_b, seq=seq, tm=512)


def _dilated_block(x2, scale, shift, gate, kv_w, in_w, out_w, ln_g, ln_b, *, batch, seq):
    w = jnp.concatenate([in_w[:, :DIL_Q_WIDTH], kv_w, in_w[:, DIL_Q_WIDTH:]], axis=1).astype(BF16)
    *qkvs, z = _b_proj(x2, scale, shift, w, batch=batch, seq=seq, tm=512)
    os_, stats = [], []
    for group in range(DIL_N_GROUPS):
        tq = min(ATTN_ROWS, seq // DIL_PATTERNS[group][1])
        o, st = _dilated_attention(qkvs[group], group, n_sub=ATTN_ROWS // tq, tq=tq)
        os_.append(o)
        stats.append(st)
    return _merge_ln(os_, stats, z, out_w.astype(BF16), x2, gate, ln_g, ln_b, seq=seq, tm=512)


def kernel(x, c, ada_w, ada_b, ln_g, ln_b, a_in_w, a_conv_w, a_conv_b, a_dt_bias, a_A_log, a_D,
           a_norm_g, a_out_w, kv_w, b_in_w, b_out_w):
    batch, seq, d = x.shape
    x2 = x.reshape(batch * seq, d)

    c_pad = jnp.pad(c, ((0, 2 * SUBLANES - batch), (0, 0)))
    mod = _adaln(c_pad, ada_w, ada_b)[:, :batch]
    shift = mod[:, :, None, 0:d]
    scale = mod[:, :, None, d:2 * d]
    gate = mod[:, :, None, 2 * d:3 * d]

    x2 = _ssd_block(x2, scale[0], shift[0], gate[0], a_in_w[0], a_conv_w[0], a_conv_b[0], a_dt_bias[0],
                    a_A_log[0], a_D[0], a_norm_g[0], a_out_w[0], ln_g[0], ln_b[0], batch=batch, seq=seq)
    x2 = _dilated_block(x2, scale[1], shift[1], gate[1], kv_w, b_in_w[0], b_out_w[0], ln_g[1], ln_b[1],
                        batch=batch, seq=seq)
    return x2.reshape(batch, seq, d)
```

```python
import numpy as np
import jax
import jax.numpy as jnp
from jax import lax
from jax.experimental import pallas as pl
from jax.experimental.pallas import tpu as pltpu

F32 = jnp.float32
BF16 = jnp.bfloat16

D_MODEL = 2048
DEPTH = 2
LANES = 128
SUBLANES = 8

SSD_D_INNER = 2 * D_MODEL
SSD_HEAD_DIM = 64
SSD_N_HEADS = SSD_D_INNER // SSD_HEAD_DIM
SSD_N_GROUPS = 8
SSD_HEADS_PER_GROUP = SSD_N_HEADS // SSD_N_GROUPS
SSD_D_STATE = 128
SSD_CONV_W = 4
SSD_CHUNK = 256
SSD_BC_DIM = SSD_N_GROUPS * SSD_D_STATE
SSD_CONV_DIM = SSD_D_INNER + 2 * SSD_BC_DIM
SSD_GROUP_W = SSD_HEADS_PER_GROUP * SSD_HEAD_DIM

DIL_PATTERNS = ((128, 1), (512, 4), (2048, 16))
DIL_N_GROUPS = len(DIL_PATTERNS)
DIL_HEADS = 8
DIL_HEAD_DIM = 128
DIL_Q_WIDTH = DIL_N_GROUPS * DIL_HEADS * DIL_HEAD_DIM
DIL_OUT_WIDTH = DIL_HEADS * DIL_HEAD_DIM
DIL_BLOCK = 128
ATTN_ROWS = 512

DEEPNORM_ALPHA = (2 * DEPTH) ** 0.25
LN_EPS = 1e-5
RMS_EPS = 1e-5
LOG2E = 1.4426950408889634

VMEM_LIMIT = 56 * 1024 * 1024


def _params(sem):
    return pltpu.CompilerParams(dimension_semantics=sem, vmem_limit_bytes=VMEM_LIMIT)


def _silu(v):
    return v * (1.0 / (1.0 + jnp.exp(-v)))


def _split3(v):
    v1 = v.astype(BF16)
    r1 = v - v1.astype(F32)
    v2 = r1.astype(BF16)
    r2 = r1 - v2.astype(F32)
    v3 = r2.astype(BF16)
    return jnp.concatenate([v1, v2, v3], axis=1)


def _layer_norm(u, g, b):
    mu = jnp.mean(u, axis=-1, keepdims=True)
    d = u - mu
    var = jnp.mean(d * d, axis=-1, keepdims=True)
    return d * lax.rsqrt(var + LN_EPS) * g + b


def _adaln_kernel(c_ref, w_ref, b_ref, o_ref):
    s = _silu(c_ref[...]).astype(BF16)
    o_ref[...] = jnp.dot(s, w_ref[...].astype(BF16), preferred_element_type=F32) + b_ref[...]


def _adaln(c_pad, ada_w, ada_b):
    rows = c_pad.shape[0]
    n = 3 * D_MODEL
    tn = 768
    return pl.pallas_call(
        _adaln_kernel,
        grid=(DEPTH, n // tn),
        in_specs=[
            pl.BlockSpec((rows, D_MODEL), lambda l, j: (0, 0)),
            pl.BlockSpec((None, D_MODEL, tn), lambda l, j: (l, 0, j)),
            pl.BlockSpec((None, 1, tn), lambda l, j: (l, 0, j)),
        ],
        out_specs=pl.BlockSpec((None, rows, tn), lambda l, j: (l, 0, j)),
        out_shape=jax.ShapeDtypeStruct((DEPTH, rows, n), F32),
        compiler_params=_params(("parallel", "parallel")),
        name="adaln",
    )(c_pad, ada_w, ada_b.reshape(DEPTH, 1, n))


def _in_proj_kernel(x_ref, scale_ref, shift_ref, w_ref, wdt_ref, o_ref, dt_ref, h_ref):
    @pl.when(pl.program_id(1) == 0)
    def _():
        h = (x_ref[...] * (1.0 + scale_ref[...]) + shift_ref[...]).astype(BF16)
        h_ref[...] = h
        dt_ref[...] = jnp.dot(h, wdt_ref[...], preferred_element_type=F32)

    o_ref[...] = jnp.dot(h_ref[...], w_ref[...], preferred_element_type=F32)


def _in_proj(x2, scale, shift, w, wdt, *, n, seq, tm, tn):
    t, d = x2.shape
    per_batch = seq // tm
    return pl.pallas_call(
        _in_proj_kernel,
        grid=(t // tm, n // tn),
        in_specs=[
            pl.BlockSpec((tm, d), lambda i, j: (i, 0)),
            pl.BlockSpec((None, 1, d), lambda i, j: (i // per_batch, 0, 0)),
            pl.BlockSpec((None, 1, d), lambda i, j: (i // per_batch, 0, 0)),
            pl.BlockSpec((d, tn), lambda i, j: (0, j)),
            pl.BlockSpec((d, LANES), lambda i, j: (0, 0)),
        ],
        out_specs=[pl.BlockSpec((tm, tn), lambda i, j: (i, j)), pl.BlockSpec((tm, LANES), lambda i, j: (i, 0))],
        out_shape=[jax.ShapeDtypeStruct((t, n), F32), jax.ShapeDtypeStruct((t, LANES), F32)],
        scratch_shapes=[pltpu.VMEM((tm, d), BF16)],
        compiler_params=_params(("parallel", "arbitrary")),
        name="ssd_in_proj",
    )(x2, scale, shift, w, wdt)


def _conv_silu(raw, carry, w, b):
    wh = 0.5 * w
    acc = raw * wh[SSD_CONV_W - 1:SSD_CONV_W, :] + 0.5 * b
    row = lax.broadcasted_iota(jnp.int32, carry.shape, 0)
    for s in range(1, SSD_CONV_W):
        sh = pltpu.roll(raw, s, axis=0)
        prev = pltpu.roll(carry, s, axis=0)
        head = jnp.where(row < s, prev, sh[:SUBLANES])
        sh = jnp.concatenate([head, sh[SUBLANES:]], axis=0)
        acc = acc + sh * wh[SSD_CONV_W - 1 - s:SSD_CONV_W - s, :]
    return acc + acc * jnp.tanh(acc)


def _ssd_kernel(z_ref, x_ref, b_ref, c_ref, dt_ref, cw_ref, cb_ref, dtb_ref, alog_ref, dexp_ref, ng_ref,
                tril_ref, e_ref, sel_ref,
                y_ref,
                state_ref, carx_ref, carb_ref, carc_ref, v3_ref, ac3_ref):
    L = SSD_CHUNK
    gw = SSD_GROUP_W
    ns = SSD_D_STATE

    @pl.when(pl.program_id(1) == 0)
    def _():
        state_ref[...] = jnp.zeros(state_ref.shape, F32)
        carx_ref[...] = jnp.zeros(carx_ref.shape, F32)
        carb_ref[...] = jnp.zeros(carb_ref.shape, F32)
        carc_ref[...] = jnp.zeros(carc_ref.shape, F32)

    v = dt_ref[...] + dtb_ref[...]
    dt = jnp.maximum(v, 0.0) + jnp.log1p(jnp.exp(-jnp.abs(v)))
    a = dt * (-jnp.exp(alog_ref[...]))
    a3 = _split3(a)
    tril = tril_ref[...]
    acum = (jnp.dot(tril, a3[:, :LANES], preferred_element_type=F32)
            + jnp.dot(tril, a3[:, LANES:2 * LANES], preferred_element_type=F32)
            + jnp.dot(tril, a3[:, 2 * LANES:], preferred_element_type=F32))
    tail = jnp.exp(acum[L - 1:L, :] - acum)
    v3_ref[0:L, :] = _split3(dt)
    v3_ref[L:2 * L, :] = _split3(tail)
    v3_ref[2 * L:3 * L, :] = _split3(jnp.exp(acum))
    ac3_ref[...] = _split3(acum)

    li = lax.broadcasted_iota(jnp.int32, (L, L), 0)
    si = lax.broadcasted_iota(jnp.int32, (L, L), 1)
    causal = li >= si
    lane = lax.broadcasted_iota(jnp.int32, (L, LANES), 1)
    heads_per_tile = LANES // SSD_HEAD_DIM
    assert heads_per_tile == 2
    n_pairs = gw // LANES

    def group(g, carry):
        xs = pl.ds(pl.multiple_of(g * gw, gw), gw)
        bs = pl.ds(pl.multiple_of(g * ns, ns), ns)
        wb = pl.ds(pl.multiple_of(SSD_D_INNER + g * ns, ns), ns)
        wc = pl.ds(pl.multiple_of(SSD_D_INNER + SSD_BC_DIM + g * ns, ns), ns)

        ex = jnp.dot(v3_ref[...], e_ref[:, xs], preferred_element_type=F32)
        dt_e = ex[0:L]
        tail_e = ex[L:2 * L]
        eac_e = ex[2 * L:3 * L]
        acg = jnp.dot(ac3_ref[...], sel_ref[:, bs], preferred_element_type=F32) * LOG2E
        acg_t = acg.T

        x_raw = x_ref[:, xs]
        b_raw = b_ref[:, bs]
        c_raw = c_ref[:, bs]
        xg = _conv_silu(x_raw, carx_ref[:, xs], cw_ref[:, xs], cb_ref[:, xs])
        bg = _conv_silu(b_raw, carb_ref[:, bs], cw_ref[:, wb], cb_ref[:, wb])
        cg = _conv_silu(c_raw, carc_ref[:, bs], cw_ref[:, wc], cb_ref[:, wc])
        carx_ref[:, xs] = x_raw[L - SUBLANES:, :]
        carb_ref[:, bs] = b_raw[L - SUBLANES:, :]
        carc_ref[:, bs] = c_raw[L - SUBLANES:, :]

        xdt = xg * dt_e
        cb16 = cg.astype(BF16)
        bb16 = bg.astype(BF16)
        cb = lax.dot_general(cb16, bb16, (((1,), (1,)), ((), ())), preferred_element_type=F32)
        cb = jnp.where(causal, cb, 0.0)

        state = state_ref[g]
        y_off = jnp.dot(cb16, state.astype(BF16), preferred_element_type=F32) * eac_e

        pieces = []
        for pair in range(n_pairs):
            ms = []
            for sub in range(heads_per_tile):
                k = pair * heads_per_tile + sub
                seg = acg[:, k:k + 1] - acg_t[k:k + 1, :]
                ms.append((cb * jnp.exp2(jnp.minimum(seg, 0.0))).astype(BF16))
            xp = xdt[:, pair * LANES:(pair + 1) * LANES].astype(BF16)
            both = jnp.dot(jnp.concatenate(ms, axis=0), xp, preferred_element_type=F32)
            pieces.append(jnp.where(lane < SSD_HEAD_DIM, both[:L], both[L:]))
        y = jnp.concatenate(pieces, axis=1) + y_off + xg * dexp_ref[:, xs]

        xtail = (xdt * tail_e).astype(BF16)
        upd = lax.dot_general(bb16, xtail, (((0,), (0,)), ((), ())), preferred_element_type=F32)
        state_ref[g] = state * eac_e[L - 1:L, :] + upd

        hz = 0.5 * z_ref[:, xs]
        y = y * (hz + hz * jnp.tanh(hz))
        y = y * lax.rsqrt(jnp.mean(y * y, axis=-1, keepdims=True) + RMS_EPS)
        y_ref[:, xs] = (y * ng_ref[:, xs]).astype(y_ref.dtype)
        return carry

    lax.fori_loop(0, SSD_N_GROUPS, group, 0)


def _ssd_constants():
    tril = np.tril(np.ones((SSD_CHUNK, SSD_CHUNK), np.float32))
    head_of_col = np.arange(SSD_D_INNER) // SSD_HEAD_DIM
    e = (np.arange(LANES)[:, None] == head_of_col[None, :]).astype(np.float32)
    col = np.arange(SSD_N_GROUPS * LANES)
    src = np.where(col % LANES < SSD_HEADS_PER_GROUP,
                   (col // LANES) * SSD_HEADS_PER_GROUP + col % LANES, -1)
    sel = (np.arange(LANES)[:, None] == src[None, :]).astype(np.float32)
    e3 = np.concatenate([e, e, e], axis=0)
    sel3 = np.concatenate([sel, sel, sel], axis=0)
    return jnp.asarray(tril, BF16), jnp.asarray(e3, BF16), jnp.asarray(sel3, BF16)


def _ssd_scan(zx, dt, conv_w, conv_b, dt_bias, a_log, d_skip, norm_g, *, batch, seq):
    t = zx.shape[0]
    L = SSD_CHUNK
    nc = seq // L
    G = SSD_N_GROUPS
    gw = SSD_GROUP_W
    ns = SSD_D_STATE
    tril, e3, sel3 = _ssd_constants()
    pad = LANES - SSD_N_HEADS
    dtb = jnp.pad(dt_bias, (0, pad)).reshape(1, LANES)
    alog = jnp.pad(a_log, (0, pad)).reshape(1, LANES)
    dexp = jnp.repeat(d_skip, SSD_HEAD_DIM).reshape(1, SSD_D_INNER)
    ng = norm_g.reshape(1, SSD_D_INNER)
    cb2 = conv_b.reshape(1, SSD_CONV_DIM)

    zw = SSD_D_INNER
    const = lambda b, c: (0, 0)
    in_specs = [
        pl.BlockSpec((L, zw), lambda b, c: (b * nc + c, 0)),
        pl.BlockSpec((L, zw), lambda b, c: (b * nc + c, 1)),
        pl.BlockSpec((L, SSD_BC_DIM), lambda b, c: (b * nc + c, 2 * zw // SSD_BC_DIM)),
        pl.BlockSpec((L, SSD_BC_DIM), lambda b, c: (b * nc + c, 2 * zw // SSD_BC_DIM + 1)),
        pl.BlockSpec((L, LANES), lambda b, c: (b * nc + c, 0)),
        pl.BlockSpec((SSD_CONV_W, SSD_CONV_DIM), const),
        pl.BlockSpec((1, SSD_CONV_DIM), const),
        pl.BlockSpec((1, LANES), const),
        pl.BlockSpec((1, LANES), const),
        pl.BlockSpec((1, zw), const),
        pl.BlockSpec((1, zw), const),
        pl.BlockSpec((L, L), const),
        pl.BlockSpec((3 * LANES, zw), const),
        pl.BlockSpec((3 * LANES, G * LANES), const),
    ]
    return pl.pallas_call(
        _ssd_kernel,
        grid=(batch, nc),
        in_specs=in_specs,
        out_specs=pl.BlockSpec((L, zw), lambda b, c: (b * nc + c, 0)),
        out_shape=jax.ShapeDtypeStruct((t, zw), BF16),
        scratch_shapes=[
            pltpu.VMEM((G, ns, gw), F32),
            pltpu.VMEM((SUBLANES, zw), F32),
            pltpu.VMEM((SUBLANES, SSD_BC_DIM), F32),
            pltpu.VMEM((SUBLANES, SSD_BC_DIM), F32),
            pltpu.VMEM((3 * L, 3 * LANES), BF16),
            pltpu.VMEM((L, 3 * LANES), BF16),
        ],
        compiler_params=_params(("parallel", "arbitrary")),
        name="ssd_scan",
    )(zx, zx, zx, zx, dt, conv_w, cb2, dtb, alog, dexp, ng, tril, e3, sel3)


def _proj_ln_kernel(y_ref, w_ref, x_ref, gate_ref, g_ref, b_ref, o_ref):
    acc = jnp.dot(y_ref[...], w_ref[...], preferred_element_type=F32)
    u = DEEPNORM_ALPHA * x_ref[...] + (1.0 + gate_ref[...]) * acc
    o_ref[...] = _layer_norm(u, g_ref[...], b_ref[...])


def _proj_ln(y, w, x2, gate, ln_g, ln_b, *, seq, tm):
    t, kdim = y.shape
    d = w.shape[1]
    per_batch = seq // tm
    return pl.pallas_call(
        _proj_ln_kernel,
        grid=(t // tm,),
        in_specs=[
            pl.BlockSpec((tm, kdim), lambda i: (i, 0)),
            pl.BlockSpec((kdim, d), lambda i: (0, 0), pipeline_mode=pl.Buffered(1)),
            pl.BlockSpec((tm, d), lambda i: (i, 0)),
            pl.BlockSpec((None, 1, d), lambda i: (i // per_batch, 0, 0)),
            pl.BlockSpec((1, d), lambda i: (0, 0)),
            pl.BlockSpec((1, d), lambda i: (0, 0)),
        ],
        out_specs=pl.BlockSpec((tm, d), lambda i: (i, 0)),
        out_shape=jax.ShapeDtypeStruct((t, d), F32),
        compiler_params=_params(("parallel",)),
        name="out_proj_ln",
    )(y, w, x2, gate, ln_g.reshape(1, d), ln_b.reshape(1, d))


def _b_proj_kernel(x_ref, scale_ref, shift_ref, win_ref, wkv_ref, o0_ref, o1_ref, o2_ref, z_ref, h_ref, res_ref):
    j = pl.program_id(1)
    tm = x_ref.shape[0]
    n_lane_tiles = win_ref.shape[1] // LANES
    n_qkv = 3 * DIL_N_GROUPS

    @pl.when(j == 0)
    def _():
        x = x_ref[...]
        h_ref[0] = (x * (1.0 + scale_ref[...]) + shift_ref[...]).astype(BF16)
        h_ref[1] = x.astype(BF16)

    def project(h_idx, w_ref):
        acc = jnp.dot(h_ref[h_idx], w_ref[...], preferred_element_type=F32)
        for ct in range(n_lane_tiles):
            res_ref[ct] = acc[:, ct * LANES:(ct + 1) * LANES]

    shared_kv = (j >= DIL_N_GROUPS) & (j < n_qkv)
    pl.when(jnp.logical_not(shared_kv))(lambda: project(0, win_ref))
    pl.when(shared_kv)(lambda: project(1, wkv_ref))

    for group, out_ref in enumerate((o0_ref, o1_ref, o2_ref)):
        dilation = DIL_PATTERNS[group][1]
        rows = tm // dilation

        @pl.when((j < n_qkv) & (j % DIL_N_GROUPS == group))
        def _(out_ref=out_ref, dilation=dilation, rows=rows):
            for r in range(dilation):
                for ct in range(n_lane_tiles):
                    out_ref[r, :, ct * LANES:(ct + 1) * LANES] = (
                        res_ref[ct, pl.ds(r, rows, stride=dilation), :].astype(BF16))

    @pl.when(j == n_qkv)
    def _():
        for ct in range(n_lane_tiles):
            z_ref[:, ct * LANES:(ct + 1) * LANES] = res_ref[ct]


def _b_proj(x2, scale, shift, w_in, w_kv, *, batch, seq, tm):
    t, d = x2.shape
    ow = DIL_OUT_WIDTH
    per_batch = seq // tm
    n_qkv = 3 * DIL_N_GROUPS
    out_specs, out_shape = [], []
    for _, dilation in DIL_PATTERNS:
        out_specs.append(pl.BlockSpec(
            (None, dilation, tm // dilation, ow),
            lambda i, j: (i // per_batch, 0, i % per_batch, jnp.minimum(j // DIL_N_GROUPS, 2))))
        out_shape.append(jax.ShapeDtypeStruct((batch, dilation, seq // dilation, 3 * ow), BF16))
    out_specs.append(pl.BlockSpec((tm, ow), lambda i, j: (i, 0)))
    out_shape.append(jax.ShapeDtypeStruct((t, ow), F32))
    return pl.pallas_call(
        _b_proj_kernel,
        grid=(t // tm, n_qkv + 1),
        in_specs=[
            pl.BlockSpec((tm, d), lambda i, j: (i, 0)),
            pl.BlockSpec((None, 1, d), lambda i, j: (i // per_batch, 0, 0)),
            pl.BlockSpec((None, 1, d), lambda i, j: (i // per_batch, 0, 0)),
            pl.BlockSpec((d, ow), lambda i, j: (0, jnp.where(j < n_qkv, jnp.minimum(j, DIL_N_GROUPS - 1),
                                                              DIL_N_GROUPS))),
            pl.BlockSpec((d, ow), lambda i, j: (0, jnp.clip(j - DIL_N_GROUPS, 0, 2 * DIL_N_GROUPS - 1))),
        ],
        out_specs=out_specs,
        out_shape=out_shape,
        scratch_shapes=[pltpu.VMEM((2, tm, d), BF16), pltpu.VMEM((ow // LANES, tm, LANES), F32)],
        compiler_params=_params(("parallel", "arbitrary")),
        name="dilated_in_proj",
    )(x2, scale, shift, w_in, w_kv)


def _attn_bias(group):
    _, dilation = DIL_PATTERNS[group]
    n_all = DIL_N_GROUPS * DIL_HEADS
    slopes = 2.0 ** (-8.0 * np.arange(1, n_all + 1) / n_all)
    slopes = slopes.reshape(DIL_N_GROUPS, DIL_HEADS)[group].astype(np.float32)
    qi = np.arange(DIL_BLOCK)[:, None]
    kj = np.arange(2 * DIL_BLOCK)[None, :]
    delta = qi + DIL_BLOCK - kj
    valid = (delta >= 0) & (delta <= DIL_BLOCK)
    alibi = -slopes[:, None, None] * (delta * dilation).astype(np.float32)[None]
    return jnp.asarray(np.where(valid[None], alibi * np.float32(LOG2E), -np.inf).astype(np.float32))


def _attn_kernel(q_ref, k_ref, v_ref, kp_ref, vp_ref, bias_ref, o_ref, st_ref):
    first = pl.program_id(2) == 0
    blk = DIL_BLOCK
    n_sub, tq = q_ref.shape[0], q_ref.shape[1]
    scale = DIL_HEAD_DIM ** -0.5 * LOG2E
    dn = (((1,), (1,)), ((), ()))
    lane = lax.broadcasted_iota(jnp.int32, (blk, LANES), 1)
    kcol = lax.broadcasted_iota(jnp.int32, (blk, 2 * blk), 1)
    no_prev = first & (kcol < blk)
    for r in range(n_sub):
        for sb in range(tq // blk):
            rows = slice(sb * blk, (sb + 1) * blk)
            keys = slice((sb - 1) * blk, (sb + 1) * blk)
            m_tile = jnp.zeros((blk, LANES), F32)
            den_tile = jnp.ones((blk, LANES), F32)
            ss = []
            for h in range(DIL_HEADS):
                cols = slice(h * DIL_HEAD_DIM, (h + 1) * DIL_HEAD_DIM)
                q = q_ref[r, rows, cols]
                if sb == 0:
                    s = jnp.concatenate(
                        [lax.dot_general(q, kp_ref[r, :, cols], dn, preferred_element_type=F32),
                         lax.dot_general(q, k_ref[r, rows, cols], dn, preferred_element_type=F32)], axis=1)
                else:
                    s = lax.dot_general(q, k_ref[r, keys, cols], dn, preferred_element_type=F32)
                ss.append(s)
            ps = []
            for h in range(DIL_HEADS):
                bias = bias_ref[h]
                if sb == 0:
                    bias = jnp.where(no_prev, -jnp.inf, bias)
                s = ss[h] * scale + bias
                m = jnp.max(s, axis=-1, keepdims=True)
                p = jnp.exp2(s - m)
                den = jnp.sum(p, axis=-1, keepdims=True)
                ps.append(p.astype(BF16))
                m_tile = jnp.where(lane == h, m, m_tile)
                den_tile = jnp.where(lane == h, den, den_tile)
            for h in range(DIL_HEADS):
                cols = slice(h * DIL_HEAD_DIM, (h + 1) * DIL_HEAD_DIM)
                p16 = ps[h]
                if sb == 0:
                    o = (jnp.dot(p16[:, :blk], vp_ref[r, :, cols], preferred_element_type=F32)
                         + jnp.dot(p16[:, blk:], v_ref[r, rows, cols], preferred_element_type=F32))
                else:
                    o = jnp.dot(p16, v_ref[r, keys, cols], preferred_element_type=F32)
                o_ref[r, rows, cols] = o
            st_ref[r, rows, :LANES] = (m_tile + jnp.log2(den_tile)) * (1.0 / LOG2E)
            st_ref[r, rows, LANES:] = 1.0 / den_tile


def _dilated_attention(qkv, group, *, n_sub, tq):
    batch, dilation, m, _ = qkv.shape
    assert DIL_PATTERNS[group][0] // dilation == DIL_BLOCK and m % tq == 0 and dilation % n_sub == 0
    w = DIL_OUT_WIDTH
    blk = DIL_BLOCK
    sub = tq // blk

    def prev(i):
        return jnp.maximum(i * sub - 1, 0)

    return pl.pallas_call(
        _attn_kernel,
        grid=(batch, dilation // n_sub, m // tq),
        in_specs=[
            pl.BlockSpec((None, n_sub, tq, w), lambda b, r, i: (b, r, i, 0)),
            pl.BlockSpec((None, n_sub, tq, w), lambda b, r, i: (b, r, i, 1)),
            pl.BlockSpec((None, n_sub, tq, w), lambda b, r, i: (b, r, i, 2)),
            pl.BlockSpec((None, n_sub, blk, w), lambda b, r, i: (b, r, prev(i), 1)),
            pl.BlockSpec((None, n_sub, blk, w), lambda b, r, i: (b, r, prev(i), 2)),
            pl.BlockSpec((DIL_HEADS, blk, 2 * blk), lambda b, r, i: (0, 0, 0)),
        ],
        out_specs=[
            pl.BlockSpec((None, n_sub, tq, w), lambda b, r, i: (b, r, i, 0)),
            pl.BlockSpec((None, n_sub, tq, 2 * LANES), lambda b, r, i: (b, r, i, 0)),
        ],
        out_shape=[
            jax.ShapeDtypeStruct((batch, dilation, m, w), F32),
            jax.ShapeDtypeStruct((batch, dilation, m, 2 * LANES), F32),
        ],
        compiler_params=_params(("parallel", "parallel", "arbitrary")),
        name=f"dilated_attn_{group}",
    )(qkv, qkv, qkv, qkv, qkv, _attn_bias(group))


def _merge_ln_kernel(o0_ref, o1_ref, o2_ref, s0_ref, s1_ref, s2_ref, z_ref, w_ref, x_ref,
                     gate_ref, g_ref, b_ref, out_ref, on_ref, sn_ref):
    tm = x_ref.shape[0]
    assert DIL_PATTERNS[0][1] == 1
    for g, (o_ref, s_ref) in ((1, (o1_ref, s1_ref)), (2, (o2_ref, s2_ref))):
        dilation = DIL_PATTERNS[g][1]
        rows = tm // dilation
        for r in range(dilation):
            for half in range(2):
                sn_ref[g - 1, half, pl.ds(r, rows, stride=dilation), :] = s_ref[r, :, half * LANES:(half + 1) * LANES]
            for h in range(DIL_HEADS):
                on_ref[g - 1, h, pl.ds(r, rows, stride=dilation), :] = (
                    o_ref[r, :, h * DIL_HEAD_DIM:(h + 1) * DIL_HEAD_DIM])
    l0, l1, l2 = s0_ref[0, :, :LANES], sn_ref[0, 0], sn_ref[1, 0]
    mx = jnp.maximum(jnp.maximum(l0, l1), l2)
    e0, e1, e2 = jnp.exp(l0 - mx), jnp.exp(l1 - mx), jnp.exp(l2 - mx)
    inv = 1.0 / (e0 + e1 + e2)
    c0 = e0 * inv * s0_ref[0, :, LANES:]
    c1 = e1 * inv * sn_ref[0, 1]
    c2 = e2 * inv * sn_ref[1, 1]
    pieces = []
    for h in range(DIL_HEADS):
        pieces.append(o0_ref[0, :, h * DIL_HEAD_DIM:(h + 1) * DIL_HEAD_DIM] * c0[:, h:h + 1]
                      + on_ref[0, h] * c1[:, h:h + 1]
                      + on_ref[1, h] * c2[:, h:h + 1])
    hz = 0.5 * z_ref[...]
    o = jnp.concatenate(pieces, axis=1) * (hz + hz * jnp.tanh(hz))
    y = jnp.dot(o.astype(BF16), w_ref[...], preferred_element_type=F32)
    u = DEEPNORM_ALPHA * x_ref[...] + (1.0 + gate_ref[...]) * y
    out_ref[...] = _layer_norm(u, g_ref[...], b_ref[...])


def _merge_ln(os_, stats, z, w, x2, gate, ln_g, ln_b, *, seq, tm):
    t, d = x2.shape
    ow = DIL_OUT_WIDTH
    per_batch = seq // tm
    row = lambda i: (i, 0)

    def sub_major(width):
        return [pl.BlockSpec((None, dilation, tm // dilation, width),
                             lambda i: (i // per_batch, 0, i % per_batch, 0))
                for _, dilation in DIL_PATTERNS]

    return pl.pallas_call(
        _merge_ln_kernel,
        grid=(t // tm,),
        in_specs=sub_major(ow) + sub_major(2 * LANES) + [
            pl.BlockSpec((tm, ow), row),
            pl.BlockSpec((ow, d), lambda i: (0, 0)),
            pl.BlockSpec((tm, d), row),
            pl.BlockSpec((None, 1, d), lambda i: (i // per_batch, 0, 0)),
            pl.BlockSpec((1, d), lambda i: (0, 0)),
            pl.BlockSpec((1, d), lambda i: (0, 0)),
        ],
        out_specs=pl.BlockSpec((tm, d), row),
        out_shape=jax.ShapeDtypeStruct((t, d), F32),
        scratch_shapes=[pltpu.VMEM((DIL_N_GROUPS - 1, DIL_HEADS, tm, LANES), F32),
                        pltpu.VMEM((DIL_N_GROUPS - 1, 2, tm, LANES), F32)],
        compiler_params=_params(("parallel",)),
        name="merge_out_proj_ln",
    )(*os_, *stats, z, w, x2, gate, ln_g.reshape(1, d), ln_b.reshape(1, d))


def _ssd_block(x2, scale, shift, gate, in_w, conv_w, conv_b, dt_bias, a_log, d_skip, norm_g, out_w, ln_g, ln_b,
               *, batch, seq):
    w16 = in_w.astype(BF16)
    dt_w = jnp.pad(w16[:, SSD_D_INNER + SSD_CONV_DIM:], ((0, 0), (0, LANES - SSD_N_HEADS)))
    zx, dt = _in_proj(x2, scale, shift, w16, dt_w, n=SSD_D_INNER + SSD_CONV_DIM, seq=seq, tm=1024, tn=1024)
    y = _ssd_scan(zx, dt, conv_w, conv_b, dt_bias, a_log, d_skip, norm_g, batch=batch, seq=seq)
    return _proj_ln(y, out_w.astype(BF16), x2, gate, ln_g, ln_b, seq=seq, tm=512)


def _dilated_block(x2, scale, shift, gate, kv_w, in_w, out_w, ln_g, ln_b, *, batch, seq):
    *qkvs, z = _b_proj(x2, scale, shift, in_w.astype(BF16), kv_w.astype(BF16), batch=batch, seq=seq, tm=512)
    os_, stats = [], []
    for group in range(DIL_N_GROUPS):
        tq = min(ATTN_ROWS, seq // DIL_PATTERNS[group][1])
        o, st = _dilated_attention(qkvs[group], group, n_sub=ATTN_ROWS // tq, tq=tq)
        os_.append(o)
        stats.append(st)
    return _merge_ln(os_, stats, z, out_w.astype(BF16), x2, gate, ln_g, ln_b, seq=seq, tm=512)


def kernel(x, c, ada_w, ada_b, ln_g, ln_b, a_in_w, a_conv_w, a_conv_b, a_dt_bias, a_A_log, a_D,
           a_norm_g, a_out_w, kv_w, b_in_w, b_out_w):
    batch, seq, d = x.shape
    x2 = x.reshape(batch * seq, d)

    c_pad = jnp.pad(c, ((0, 2 * SUBLANES - batch), (0, 0)))
    mod = _adaln(c_pad, ada_w, ada_b)[:, :batch]
    shift = mod[:, :, None, 0:d]
    scale = mod[:, :, None, d:2 * d]
    gate = mod[:, :, None, 2 * d:3 * d]

    x2 = _ssd_block(x2, scale[0], shift[0], gate[0], a_in_w[0], a_conv_w[0], a_conv_b[0], a_dt_bias[0],
                    a_A_log[0], a_D[0], a_norm_g[0], a_out_w[0], ln_g[0], ln_b[0], batch=batch, seq=seq)
    x2 = _dilated_block(x2, scale[1], shift[1], gate[1], kv_w, b_in_w[0], b_out_w[0], ln_g[1], ln_b[1],
                        batch=batch, seq=seq)
    return x2.reshape(batch, seq, d)
```

```python
import numpy as np
import jax
import jax.numpy as jnp
from jax import lax
from jax.experimental import pallas as pl
from jax.experimental.pallas import tpu as pltpu

F32 = jnp.float32
BF16 = jnp.bfloat16

D_MODEL = 2048
DEPTH = 2
LANES = 128
SUBLANES = 8

SSD_D_INNER = 2 * D_MODEL
SSD_HEAD_DIM = 64
SSD_N_HEADS = SSD_D_INNER // SSD_HEAD_DIM
SSD_N_GROUPS = 8
SSD_HEADS_PER_GROUP = SSD_N_HEADS // SSD_N_GROUPS
SSD_D_STATE = 128
SSD_CONV_W = 4
SSD_CHUNK = 256
SSD_BC_DIM = SSD_N_GROUPS * SSD_D_STATE
SSD_CONV_DIM = SSD_D_INNER + 2 * SSD_BC_DIM
SSD_GROUP_W = SSD_HEADS_PER_GROUP * SSD_HEAD_DIM

DIL_PATTERNS = ((128, 1), (512, 4), (2048, 16))
DIL_N_GROUPS = len(DIL_PATTERNS)
DIL_HEADS = 8
DIL_HEAD_DIM = 128
DIL_Q_WIDTH = DIL_N_GROUPS * DIL_HEADS * DIL_HEAD_DIM
DIL_OUT_WIDTH = DIL_HEADS * DIL_HEAD_DIM
DIL_BLOCK = 128
SHUFFLE_STRIDE = 4
ATTN_ROWS = 512

DEEPNORM_ALPHA = (2 * DEPTH) ** 0.25
LN_EPS = 1e-5
RMS_EPS = 1e-5
LOG2E = 1.4426950408889634

VMEM_LIMIT = 56 * 1024 * 1024


def _params(sem):
    return pltpu.CompilerParams(dimension_semantics=sem, vmem_limit_bytes=VMEM_LIMIT)


def _silu(v):
    return v * (1.0 / (1.0 + jnp.exp(-v)))


def _split3(v):
    v1 = v.astype(BF16)
    r1 = v - v1.astype(F32)
    v2 = r1.astype(BF16)
    r2 = r1 - v2.astype(F32)
    v3 = r2.astype(BF16)
    return jnp.concatenate([v1, v2, v3], axis=1)


def _layer_norm(u, g, b):
    mu = jnp.mean(u, axis=-1, keepdims=True)
    d = u - mu
    var = jnp.mean(d * d, axis=-1, keepdims=True)
    return d * lax.rsqrt(var + LN_EPS) * g + b


def _adaln_kernel(c_ref, w_ref, b_ref, o_ref):
    s = _silu(c_ref[...]).astype(BF16)
    o_ref[...] = jnp.dot(s, w_ref[...].astype(BF16), preferred_element_type=F32) + b_ref[...]


def _adaln(c_pad, ada_w, ada_b):
    rows = c_pad.shape[0]
    n = 3 * D_MODEL
    tn = 768
    return pl.pallas_call(
        _adaln_kernel,
        grid=(DEPTH, n // tn),
        in_specs=[
            pl.BlockSpec((rows, D_MODEL), lambda l, j: (0, 0)),
            pl.BlockSpec((None, D_MODEL, tn), lambda l, j: (l, 0, j)),
            pl.BlockSpec((None, 1, tn), lambda l, j: (l, 0, j)),
        ],
        out_specs=pl.BlockSpec((None, rows, tn), lambda l, j: (l, 0, j)),
        out_shape=jax.ShapeDtypeStruct((DEPTH, rows, n), F32),
        compiler_params=_params(("parallel", "parallel")),
        name="adaln",
    )(c_pad, ada_w, ada_b.reshape(DEPTH, 1, n))


def _in_proj_kernel(x_ref, scale_ref, shift_ref, w_ref, wdt_ref, o_ref, dt_ref, h_ref):
    @pl.when(pl.program_id(1) == 0)
    def _():
        h = (x_ref[...] * (1.0 + scale_ref[...]) + shift_ref[...]).astype(BF16)
        h_ref[...] = h
        dt_ref[...] = jnp.dot(h, wdt_ref[...], preferred_element_type=F32)

    o_ref[...] = jnp.dot(h_ref[...], w_ref[...], preferred_element_type=F32)


def _in_proj(x2, scale, shift, w, wdt, *, n, seq, tm, tn):
    t, d = x2.shape
    per_batch = seq // tm
    return pl.pallas_call(
        _in_proj_kernel,
        grid=(t // tm, n // tn),
        in_specs=[
            pl.BlockSpec((tm, d), lambda i, j: (i, 0)),
            pl.BlockSpec((None, 1, d), lambda i, j: (i // per_batch, 0, 0)),
            pl.BlockSpec((None, 1, d), lambda i, j: (i // per_batch, 0, 0)),
            pl.BlockSpec((d, tn), lambda i, j: (0, j)),
            pl.BlockSpec((d, LANES), lambda i, j: (0, 0)),
        ],
        out_specs=[pl.BlockSpec((tm, tn), lambda i, j: (i, j)), pl.BlockSpec((tm, LANES), lambda i, j: (i, 0))],
        out_shape=[jax.ShapeDtypeStruct((t, n), F32), jax.ShapeDtypeStruct((t, LANES), F32)],
        scratch_shapes=[pltpu.VMEM((tm, d), BF16)],
        compiler_params=_params(("parallel", "arbitrary")),
        name="ssd_in_proj",
    )(x2, scale, shift, w, wdt)


def _conv_silu(raw, carry, w, b):
    wh = 0.5 * w
    acc = raw * wh[SSD_CONV_W - 1:SSD_CONV_W, :] + 0.5 * b
    row = lax.broadcasted_iota(jnp.int32, carry.shape, 0)
    for s in range(1, SSD_CONV_W):
        sh = pltpu.roll(raw, s, axis=0)
        prev = pltpu.roll(carry, s, axis=0)
        head = jnp.where(row < s, prev, sh[:SUBLANES])
        sh = jnp.concatenate([head, sh[SUBLANES:]], axis=0)
        acc = acc + sh * wh[SSD_CONV_W - 1 - s:SSD_CONV_W - s, :]
    return acc + acc * jnp.tanh(acc)


def _ssd_kernel(z_ref, x_ref, b_ref, c_ref, dt_ref, cw_ref, cb_ref, dtb_ref, alog_ref, dexp_ref, ng_ref,
                tril_ref, e_ref, sel_ref,
                y_ref,
                state_ref, carx_ref, carb_ref, carc_ref, v3_ref, ac3_ref, ybuf_ref):
    L = SSD_CHUNK
    gw = SSD_GROUP_W
    ns = SSD_D_STATE

    @pl.when(pl.program_id(1) == 0)
    def _():
        state_ref[...] = jnp.zeros(state_ref.shape, F32)
        carx_ref[...] = jnp.zeros(carx_ref.shape, F32)
        carb_ref[...] = jnp.zeros(carb_ref.shape, F32)
        carc_ref[...] = jnp.zeros(carc_ref.shape, F32)

    v = dt_ref[...] + dtb_ref[...]
    dt = jnp.maximum(v, 0.0) + jnp.log1p(jnp.exp(-jnp.abs(v)))
    a = dt * (-jnp.exp(alog_ref[...]))
    a3 = _split3(a)
    tril = tril_ref[...]
    acum = (jnp.dot(tril, a3[:, :LANES], preferred_element_type=F32)
            + jnp.dot(tril, a3[:, LANES:2 * LANES], preferred_element_type=F32)
            + jnp.dot(tril, a3[:, 2 * LANES:], preferred_element_type=F32))
    tail = jnp.exp(acum[L - 1:L, :] - acum)
    v3_ref[0:L, :] = _split3(dt)
    v3_ref[L:2 * L, :] = _split3(tail)
    v3_ref[2 * L:3 * L, :] = _split3(jnp.exp(acum))
    ac3_ref[...] = _split3(acum)

    li = lax.broadcasted_iota(jnp.int32, (L, L), 0)
    si = lax.broadcasted_iota(jnp.int32, (L, L), 1)
    causal = li >= si
    lane = lax.broadcasted_iota(jnp.int32, (L, LANES), 1)
    heads_per_tile = LANES // SSD_HEAD_DIM
    assert heads_per_tile == 2
    n_pairs = gw // LANES

    def group(g, carry):
        xs = pl.ds(pl.multiple_of(g * gw, gw), gw)
        bs = pl.ds(pl.multiple_of(g * ns, ns), ns)
        wb = pl.ds(pl.multiple_of(SSD_D_INNER + g * ns, ns), ns)
        wc = pl.ds(pl.multiple_of(SSD_D_INNER + SSD_BC_DIM + g * ns, ns), ns)

        ex = jnp.dot(v3_ref[...], e_ref[:, xs], preferred_element_type=F32)
        dt_e = ex[0:L]
        tail_e = ex[L:2 * L]
        eac_e = ex[2 * L:3 * L]
        acg = jnp.dot(ac3_ref[...], sel_ref[:, bs], preferred_element_type=F32) * LOG2E
        acg_t = acg.T

        b_raw = b_ref[:, bs]
        c_raw = c_ref[:, bs]
        bg = _conv_silu(b_raw, carb_ref[:, bs], cw_ref[:, wb], cb_ref[:, wb])
        cg = _conv_silu(c_raw, carc_ref[:, bs], cw_ref[:, wc], cb_ref[:, wc])
        carb_ref[:, bs] = b_raw[L - SUBLANES:, :]
        carc_ref[:, bs] = c_raw[L - SUBLANES:, :]
        cb16 = cg.astype(BF16)
        bb16 = bg.astype(BF16)
        cb = lax.dot_general(cb16, bb16, (((1,), (1,)), ((), ())), preferred_element_type=F32)
        cb = jnp.where(causal, cb, 0.0)

        state = state_ref[g]
        y_state = jnp.dot(cb16, state.astype(BF16), preferred_element_type=F32)

        ssq = jnp.zeros((L, LANES), F32)
        for pair in range(n_pairs):
            ps = slice(pair * LANES, (pair + 1) * LANES)
            cs = pl.ds(pl.multiple_of(g * gw + pair * LANES, LANES), LANES)
            x_raw = x_ref[:, cs]
            xg = _conv_silu(x_raw, carx_ref[:, cs], cw_ref[:, cs], cb_ref[:, cs])
            carx_ref[:, cs] = x_raw[L - SUBLANES:, :]
            xdt = xg * dt_e[:, ps]
            ms = []
            for sub in range(heads_per_tile):
                k = pair * heads_per_tile + sub
                seg = acg[:, k:k + 1] - acg_t[k:k + 1, :]
                ms.append((cb * jnp.exp2(jnp.minimum(seg, 0.0))).astype(BF16))
            both = jnp.dot(jnp.concatenate(ms, axis=0), xdt.astype(BF16), preferred_element_type=F32)
            y = (jnp.where(lane < SSD_HEAD_DIM, both[:L], both[L:]) + y_state[:, ps] * eac_e[:, ps]
                 + xg * dexp_ref[:, cs])
            xtail = (xdt * tail_e[:, ps]).astype(BF16)
            upd = lax.dot_general(bb16, xtail, (((0,), (0,)), ((), ())), preferred_element_type=F32)
            state_ref[g, :, ps] = state[:, ps] * eac_e[L - 1:L, ps] + upd
            hz = 0.5 * z_ref[:, cs]
            y = y * (hz + hz * jnp.tanh(hz))
            ssq = ssq + y * y
            ybuf_ref[:, ps] = y
        r = lax.rsqrt(jnp.sum(ssq, axis=-1, keepdims=True) * (1.0 / gw) + RMS_EPS)
        y_ref[:, xs] = (ybuf_ref[...] * r * ng_ref[:, xs]).astype(y_ref.dtype)
        return carry

    lax.fori_loop(0, SSD_N_GROUPS, group, 0)


def _ssd_constants():
    tril = np.tril(np.ones((SSD_CHUNK, SSD_CHUNK), np.float32))
    head_of_col = np.arange(SSD_D_INNER) // SSD_HEAD_DIM
    e = (np.arange(LANES)[:, None] == head_of_col[None, :]).astype(np.float32)
    col = np.arange(SSD_N_GROUPS * LANES)
    src = np.where(col % LANES < SSD_HEADS_PER_GROUP,
                   (col // LANES) * SSD_HEADS_PER_GROUP + col % LANES, -1)
    sel = (np.arange(LANES)[:, None] == src[None, :]).astype(np.float32)
    e3 = np.concatenate([e, e, e], axis=0)
    sel3 = np.concatenate([sel, sel, sel], axis=0)
    return jnp.asarray(tril, BF16), jnp.asarray(e3, BF16), jnp.asarray(sel3, BF16)


def _ssd_scan(zx, dt, conv_w, conv_b, dt_bias, a_log, d_skip, norm_g, *, batch, seq):
    t = zx.shape[0]
    L = SSD_CHUNK
    nc = seq // L
    G = SSD_N_GROUPS
    gw = SSD_GROUP_W
    ns = SSD_D_STATE
    tril, e3, sel3 = _ssd_constants()
    pad = LANES - SSD_N_HEADS
    dtb = jnp.pad(dt_bias, (0, pad)).reshape(1, LANES)
    alog = jnp.pad(a_log, (0, pad)).reshape(1, LANES)
    dexp = jnp.repeat(d_skip, SSD_HEAD_DIM).reshape(1, SSD_D_INNER)
    ng = norm_g.reshape(1, SSD_D_INNER)
    cb2 = conv_b.reshape(1, SSD_CONV_DIM)

    zw = SSD_D_INNER
    const = lambda b, c: (0, 0)
    in_specs = [
        pl.BlockSpec((L, zw), lambda b, c: (b * nc + c, 0)),
        pl.BlockSpec((L, zw), lambda b, c: (b * nc + c, 1)),
        pl.BlockSpec((L, SSD_BC_DIM), lambda b, c: (b * nc + c, 2 * zw // SSD_BC_DIM)),
        pl.BlockSpec((L, SSD_BC_DIM), lambda b, c: (b * nc + c, 2 * zw // SSD_BC_DIM + 1)),
        pl.BlockSpec((L, LANES), lambda b, c: (b * nc + c, 0)),
        pl.BlockSpec((SSD_CONV_W, SSD_CONV_DIM), const),
        pl.BlockSpec((1, SSD_CONV_DIM), const),
        pl.BlockSpec((1, LANES), const),
        pl.BlockSpec((1, LANES), const),
        pl.BlockSpec((1, zw), const),
        pl.BlockSpec((1, zw), const),
        pl.BlockSpec((L, L), const),
        pl.BlockSpec((3 * LANES, zw), const),
        pl.BlockSpec((3 * LANES, G * LANES), const),
    ]
    return pl.pallas_call(
        _ssd_kernel,
        grid=(batch, nc),
        in_specs=in_specs,
        out_specs=pl.BlockSpec((L, zw), lambda b, c: (b * nc + c, 0)),
        out_shape=jax.ShapeDtypeStruct((t, zw), BF16),
        scratch_shapes=[
            pltpu.VMEM((G, ns, gw), F32),
            pltpu.VMEM((SUBLANES, zw), F32),
            pltpu.VMEM((SUBLANES, SSD_BC_DIM), F32),
            pltpu.VMEM((SUBLANES, SSD_BC_DIM), F32),
            pltpu.VMEM((3 * L, 3 * LANES), BF16),
            pltpu.VMEM((L, 3 * LANES), BF16),
            pltpu.VMEM((L, gw), F32),
        ],
        compiler_params=_params(("parallel", "arbitrary")),
        name="ssd_scan",
    )(zx, zx, zx, zx, dt, conv_w, cb2, dtb, alog, dexp, ng, tril, e3, sel3)


def _proj_ln_kernel(y_ref, w_ref, x_ref, gate_ref, g_ref, b_ref, o_ref):
    acc = jnp.dot(y_ref[...], w_ref[...], preferred_element_type=F32)
    u = DEEPNORM_ALPHA * x_ref[...] + (1.0 + gate_ref[...]) * acc
    o_ref[...] = _layer_norm(u, g_ref[...], b_ref[...])


def _proj_ln(y, w, x2, gate, ln_g, ln_b, *, seq, tm):
    t, kdim = y.shape
    d = w.shape[1]
    per_batch = seq // tm
    return pl.pallas_call(
        _proj_ln_kernel,
        grid=(t // tm,),
        in_specs=[
            pl.BlockSpec((tm, kdim), lambda i: (i, 0)),
            pl.BlockSpec((kdim, d), lambda i: (0, 0), pipeline_mode=pl.Buffered(1)),
            pl.BlockSpec((tm, d), lambda i: (i, 0)),
            pl.BlockSpec((None, 1, d), lambda i: (i // per_batch, 0, 0)),
            pl.BlockSpec((1, d), lambda i: (0, 0)),
            pl.BlockSpec((1, d), lambda i: (0, 0)),
        ],
        out_specs=pl.BlockSpec((tm, d), lambda i: (i, 0)),
        out_shape=jax.ShapeDtypeStruct((t, d), F32),
        compiler_params=_params(("parallel",)),
        name="out_proj_ln",
    )(y, w, x2, gate, ln_g.reshape(1, d), ln_b.reshape(1, d))


def _b_proj_kernel(x_ref, scale_ref, shift_ref, w_ref, o0_ref, o1_ref, o2_ref, z_ref, h_ref, res_ref, tmp_ref):
    j = pl.program_id(1)
    tm = x_ref.shape[0]
    n_qkv = 3 * DIL_N_GROUPS
    tiles = 2
    n_chunks = w_ref.shape[1] // (tiles * LANES)

    @pl.when(j == 0)
    def _():
        x = x_ref[...]
        h_ref[0] = (x * (1.0 + scale_ref[...]) + shift_ref[...]).astype(BF16)
        h_ref[1] = x.astype(BF16)

    plain = ((j >= DIL_N_GROUPS) & (j < n_qkv)).astype(jnp.int32)

    def matmul_chunk(c):
        cols = slice(c * tiles * LANES, (c + 1) * tiles * LANES)
        acc = jnp.dot(h_ref[plain], w_ref[:, cols], preferred_element_type=F32)
        for t in range(tiles):
            res_ref[c * tiles + t] = acc[:, t * LANES:(t + 1) * LANES]

    def pipelined(emit_tile):
        matmul_chunk(0)
        for c in range(1, n_chunks + 1):
            if c < n_chunks:
                matmul_chunk(c)
            for ct in range((c - 1) * tiles, c * tiles):
                emit_tile(ct)

    for group, out_ref in enumerate((o0_ref, o1_ref, o2_ref)):
        dilation = DIL_PATTERNS[group][1]
        rows = tm // dilation

        def emit(ct, out_ref=out_ref, dilation=dilation, rows=rows):
            cols = slice(ct * LANES, (ct + 1) * LANES)
            if dilation <= SHUFFLE_STRIDE:
                for r in range(dilation):
                    out_ref[r, :, cols] = res_ref[ct, pl.ds(r, rows, stride=dilation), :].astype(BF16)
            else:
                assert dilation == SHUFFLE_STRIDE * SHUFFLE_STRIDE
                tmp = tmp_ref.at[ct % 2]
                sub_rows = tm // SHUFFLE_STRIDE
                for r_lo in range(SHUFFLE_STRIDE):
                    tmp[r_lo] = res_ref[ct, pl.ds(r_lo, sub_rows, stride=SHUFFLE_STRIDE), :]
                for r_lo in range(SHUFFLE_STRIDE):
                    for r_hi in range(SHUFFLE_STRIDE):
                        out_ref[r_lo + SHUFFLE_STRIDE * r_hi, :, cols] = (
                            tmp[r_lo, pl.ds(r_hi, rows, stride=SHUFFLE_STRIDE), :].astype(BF16))

        pl.when((j < n_qkv) & (j % DIL_N_GROUPS == group))(lambda emit=emit: pipelined(emit))

    def emit_gate(ct):
        z_ref[:, ct * LANES:(ct + 1) * LANES] = res_ref[ct]

    pl.when(j == n_qkv)(lambda: pipelined(emit_gate))


def _b_proj(x2, scale, shift, w, *, batch, seq, tm):
    t, d = x2.shape
    ow = DIL_OUT_WIDTH
    per_batch = seq // tm
    n_qkv = 3 * DIL_N_GROUPS
    out_specs, out_shape = [], []
    for _, dilation in DIL_PATTERNS:
        out_specs.append(pl.BlockSpec(
            (None, dilation, tm // dilation, ow),
            lambda i, j: (i // per_batch, 0, i % per_batch, jnp.minimum(j // DIL_N_GROUPS, 2))))
        out_shape.append(jax.ShapeDtypeStruct((batch, dilation, seq // dilation, 3 * ow), BF16))
    out_specs.append(pl.BlockSpec((tm, ow), lambda i, j: (i, 0)))
    out_shape.append(jax.ShapeDtypeStruct((t, ow), F32))
    return pl.pallas_call(
        _b_proj_kernel,
        grid=(t // tm, n_qkv + 1),
        in_specs=[
            pl.BlockSpec((tm, d), lambda i, j: (i, 0)),
            pl.BlockSpec((None, 1, d), lambda i, j: (i // per_batch, 0, 0)),
            pl.BlockSpec((None, 1, d), lambda i, j: (i // per_batch, 0, 0)),
            pl.BlockSpec((d, ow), lambda i, j: (0, j)),
        ],
        out_specs=out_specs,
        out_shape=out_shape,
        scratch_shapes=[pltpu.VMEM((2, tm, d), BF16), pltpu.VMEM((ow // LANES, tm, LANES), F32),
                        pltpu.VMEM((2, SHUFFLE_STRIDE, tm // SHUFFLE_STRIDE, LANES), F32)],
        compiler_params=_params(("parallel", "arbitrary")),
        name="dilated_in_proj",
    )(x2, scale, shift, w)


def _attn_bias(group):
    _, dilation = DIL_PATTERNS[group]
    n_all = DIL_N_GROUPS * DIL_HEADS
    slopes = 2.0 ** (-8.0 * np.arange(1, n_all + 1) / n_all)
    slopes = slopes.reshape(DIL_N_GROUPS, DIL_HEADS)[group].astype(np.float32)
    qi = np.arange(DIL_BLOCK)[:, None]
    kj = np.arange(2 * DIL_BLOCK)[None, :]
    delta = qi + DIL_BLOCK - kj
    valid = (delta >= 0) & (delta <= DIL_BLOCK)
    alibi = -slopes[:, None, None] * (delta * dilation).astype(np.float32)[None]
    return jnp.asarray(np.where(valid[None], alibi * np.float32(LOG2E), -np.inf).astype(np.float32))


def _attn_kernel(q_ref, k_ref, v_ref, kp_ref, vp_ref, bias_ref, o_ref, st_ref):
    first = pl.program_id(2) == 0
    blk = DIL_BLOCK
    n_sub, tq = q_ref.shape[0], q_ref.shape[1]
    scale = DIL_HEAD_DIM ** -0.5 * LOG2E
    dn = (((1,), (1,)), ((), ()))
    lane = lax.broadcasted_iota(jnp.int32, (blk, LANES), 1)
    kcol = lax.broadcasted_iota(jnp.int32, (blk, 2 * blk), 1)
    no_prev = first & (kcol < blk)
    for r in range(n_sub):
        for sb in range(tq // blk):
            rows = slice(sb * blk, (sb + 1) * blk)
            keys = slice((sb - 1) * blk, (sb + 1) * blk)
            m_tile = jnp.zeros((blk, LANES), F32)
            den_tile = jnp.ones((blk, LANES), F32)
            ss = []
            for h in range(DIL_HEADS):
                cols = slice(h * DIL_HEAD_DIM, (h + 1) * DIL_HEAD_DIM)
                q = q_ref[r, rows, cols]
                if sb == 0:
                    s = jnp.concatenate(
                        [lax.dot_general(q, kp_ref[r, :, cols], dn, preferred_element_type=F32),
                         lax.dot_general(q, k_ref[r, rows, cols], dn, preferred_element_type=F32)], axis=1)
                else:
                    s = lax.dot_general(q, k_ref[r, keys, cols], dn, preferred_element_type=F32)
                ss.append(s)
            ps = []
            for h in range(DIL_HEADS):
                bias = bias_ref[h]
                if sb == 0:
                    bias = jnp.where(no_prev, -jnp.inf, bias)
                s = ss[h] * scale + bias
                m = jnp.max(s, axis=-1, keepdims=True)
                p = jnp.exp2(s - m)
                den = jnp.sum(p, axis=-1, keepdims=True)
                ps.append(p.astype(BF16))
                m_tile = jnp.where(lane == h, m, m_tile)
                den_tile = jnp.where(lane == h, den, den_tile)
            for h in range(DIL_HEADS):
                cols = slice(h * DIL_HEAD_DIM, (h + 1) * DIL_HEAD_DIM)
                p16 = ps[h]
                if sb == 0:
                    o = (jnp.dot(p16[:, :blk], vp_ref[r, :, cols], preferred_element_type=F32)
                         + jnp.dot(p16[:, blk:], v_ref[r, rows, cols], preferred_element_type=F32))
                else:
                    o = jnp.dot(p16, v_ref[r, keys, cols], preferred_element_type=F32)
                o_ref[r, rows, cols] = o
            st_ref[r, rows, :LANES] = (m_tile + jnp.log2(den_tile)) * (1.0 / LOG2E)
            st_ref[r, rows, LANES:] = 1.0 / den_tile


def _dilated_attention(qkv, group, *, n_sub, tq):
    batch, dilation, m, _ = qkv.shape
    assert DIL_PATTERNS[group][0] // dilation == DIL_BLOCK and m % tq == 0 and dilation % n_sub == 0
    w = DIL_OUT_WIDTH
    blk = DIL_BLOCK
    sub = tq // blk

    def prev(i):
        return jnp.maximum(i * sub - 1, 0)

    return pl.pallas_call(
        _attn_kernel,
        grid=(batch, dilation // n_sub, m // tq),
        in_specs=[
            pl.BlockSpec((None, n_sub, tq, w), lambda b, r, i: (b, r, i, 0)),
            pl.BlockSpec((None, n_sub, tq, w), lambda b, r, i: (b, r, i, 1)),
            pl.BlockSpec((None, n_sub, tq, w), lambda b, r, i: (b, r, i, 2)),
            pl.BlockSpec((None, n_sub, blk, w), lambda b, r, i: (b, r, prev(i), 1)),
            pl.BlockSpec((None, n_sub, blk, w), lambda b, r, i: (b, r, prev(i), 2)),
            pl.BlockSpec((DIL_HEADS, blk, 2 * blk), lambda b, r, i: (0, 0, 0)),
        ],
        out_specs=[
            pl.BlockSpec((None, n_sub, tq, w), lambda b, r, i: (b, r, i, 0)),
            pl.BlockSpec((None, n_sub, tq, 2 * LANES), lambda b, r, i: (b, r, i, 0)),
        ],
        out_shape=[
            jax.ShapeDtypeStruct((batch, dilation, m, w), F32),
            jax.ShapeDtypeStruct((batch, dilation, m, 2 * LANES), F32),
        ],
        compiler_params=_params(("parallel", "parallel", "arbitrary")),
        name=f"dilated_attn_{group}",
    )(qkv, qkv, qkv, qkv, qkv, _attn_bias(group))


def _merge_ln_kernel(o0_ref, o1_ref, o2_ref, s0_ref, s1_ref, s2_ref, z_ref, w_ref, x_ref,
                     gate_ref, g_ref, b_ref, out_ref, on_ref, sn_ref, tmp_ref):
    tm = x_ref.shape[0]
    assert DIL_PATTERNS[0][1] == 1
    def scatter(dst, src_ref, cols, dilation, tmp):
        rows = tm // dilation
        if dilation <= SHUFFLE_STRIDE:
            for r in range(dilation):
                dst[pl.ds(r, rows, stride=dilation), :] = src_ref[r, :, cols]
        else:
            assert dilation == SHUFFLE_STRIDE * SHUFFLE_STRIDE
            for r_lo in range(SHUFFLE_STRIDE):
                for r_hi in range(SHUFFLE_STRIDE):
                    tmp[r_lo, pl.ds(r_hi, rows, stride=SHUFFLE_STRIDE), :] = (
                        src_ref[r_lo + SHUFFLE_STRIDE * r_hi, :, cols])
            for r_lo in range(SHUFFLE_STRIDE):
                dst[pl.ds(r_lo, tm // SHUFFLE_STRIDE, stride=SHUFFLE_STRIDE), :] = tmp[r_lo]

    n_tmp = tmp_ref.shape[0]
    for g, (o_ref, s_ref) in ((1, (o1_ref, s1_ref)), (2, (o2_ref, s2_ref))):
        dilation = DIL_PATTERNS[g][1]
        for half in range(2):
            scatter(sn_ref.at[g - 1, half], s_ref, slice(half * LANES, (half + 1) * LANES), dilation,
                    tmp_ref.at[half % n_tmp])
        for h in range(DIL_HEADS):
            scatter(on_ref.at[g - 1, h], o_ref, slice(h * DIL_HEAD_DIM, (h + 1) * DIL_HEAD_DIM), dilation,
                    tmp_ref.at[h % n_tmp])
    l0, l1, l2 = s0_ref[0, :, :LANES], sn_ref[0, 0], sn_ref[1, 0]
    mx = jnp.maximum(jnp.maximum(l0, l1), l2)
    e0, e1, e2 = jnp.exp(l0 - mx), jnp.exp(l1 - mx), jnp.exp(l2 - mx)
    inv = 1.0 / (e0 + e1 + e2)
    c0 = e0 * inv * s0_ref[0, :, LANES:]
    c1 = e1 * inv * sn_ref[0, 1]
    c2 = e2 * inv * sn_ref[1, 1]
    pieces = []
    for h in range(DIL_HEADS):
        pieces.append(o0_ref[0, :, h * DIL_HEAD_DIM:(h + 1) * DIL_HEAD_DIM] * c0[:, h:h + 1]
                      + on_ref[0, h] * c1[:, h:h + 1]
                      + on_ref[1, h] * c2[:, h:h + 1])
    hz = 0.5 * z_ref[...]
    o = jnp.concatenate(pieces, axis=1) * (hz + hz * jnp.tanh(hz))
    y = jnp.dot(o.astype(BF16), w_ref[...], preferred_element_type=F32)
    u = DEEPNORM_ALPHA * x_ref[...] + (1.0 + gate_ref[...]) * y
    out_ref[...] = _layer_norm(u, g_ref[...], b_ref[...])


def _merge_ln(os_, stats, z, w, x2, gate, ln_g, ln_b, *, seq, tm):
    t, d = x2.shape
    ow = DIL_OUT_WIDTH
    per_batch = seq // tm
    row = lambda i: (i, 0)

    def sub_major(width):
        return [pl.BlockSpec((None, dilation, tm // dilation, width),
                             lambda i: (i // per_batch, 0, i % per_batch, 0))
                for _, dilation in DIL_PATTERNS]

    return pl.pallas_call(
        _merge_ln_kernel,
        grid=(t // tm,),
        in_specs=sub_major(ow) + sub_major(2 * LANES) + [
            pl.BlockSpec((tm, ow), row),
            pl.BlockSpec((ow, d), lambda i: (0, 0)),
            pl.BlockSpec((tm, d), row),
            pl.BlockSpec((None, 1, d), lambda i: (i // per_batch, 0, 0)),
            pl.BlockSpec((1, d), lambda i: (0, 0)),
            pl.BlockSpec((1, d), lambda i: (0, 0)),
        ],
        out_specs=pl.BlockSpec((tm, d), row),
        out_shape=jax.ShapeDtypeStruct((t, d), F32),
        scratch_shapes=[pltpu.VMEM((DIL_N_GROUPS - 1, DIL_HEADS, tm, LANES), F32),
                        pltpu.VMEM((DIL_N_GROUPS - 1, 2, tm, LANES), F32),
                        pltpu.VMEM((2, SHUFFLE_STRIDE, tm // SHUFFLE_STRIDE, LANES), F32)],
        compiler_params=_params(("parallel",)),
        name="merge_out_proj_ln",
    )(*os_, *stats, z, w, x2, gate, ln_g.reshape(1, d), ln_b.reshape(1, d))


def _ssd_block(x2, scale, shift, gate, in_w, conv_w, conv_b, dt_bias, a_log, d_skip, norm_g, out_w, ln_g, ln_b,
               *, batch, seq):
    w16 = in_w.astype(BF16)
    dt_w = jnp.pad(w16[:, SSD_D_INNER + SSD_CONV_DIM:], ((0, 0), (0, LANES - SSD_N_HEADS)))
    zx, dt = _in_proj(x2, scale, shift, w16, dt_w, n=SSD_D_INNER + SSD_CONV_DIM, seq=seq, tm=1024, tn=1024)
    y = _ssd_scan(zx, dt, conv_w, conv_b, dt_bias, a_log, d_skip, norm_g, batch=batch, seq=seq)
    return _proj_ln(y, out_w.astype(BF16), x2, gate, ln_g, ln_b, seq=seq, tm=512)


def _dilated_block(x2, scale, shift, gate, kv_w, in_w, out_w, ln_g, ln_b, *, batch, seq):
    w = jnp.concatenate([in_w[:, :DIL_Q_WIDTH], kv_w, in_w[:, DIL_Q_WIDTH:]], axis=1).astype(BF16)
    *qkvs, z = _b_proj(x2, scale, shift, w, batch=batch, seq=seq, tm=512)
    os_, stats = [], []
    for group in range(DIL_N_GROUPS):
        tq = min(ATTN_ROWS, seq // DIL_PATTERNS[group][1])
        o, st = _dilated_attention(qkvs[group], group, n_sub=ATTN_ROWS // tq, tq=tq)
        os_.append(o)
        stats.append(st)
    return _merge_ln(os_, stats, z, out_w.astype(BF16), x2, gate, ln_g, ln_b, seq=seq, tm=512)


def kernel(x, c, ada_w, ada_b, ln_g, ln_b, a_in_w, a_conv_w, a_conv_b, a_dt_bias, a_A_log, a_D,
           a_norm_g, a_out_w, kv_w, b_in_w, b_out_w):
    batch, seq, d = x.shape
    x2 = x.reshape(batch * seq, d)

    c_pad = jnp.pad(c, ((0, 2 * SUBLANES - batch), (0, 0)))
    mod = _adaln(c_pad, ada_w, ada_b)[:, :batch]
    shift = mod[:, :, None, 0:d]
    scale = mod[:, :, None, d:2 * d]
    gate = mod[:, :, None, 2 * d:3 * d]

    x2 = _ssd_block(x2, scale[0], shift[0], gate[0], a_in_w[0], a_conv_w[0], a_conv_b[0], a_dt_bias[0],
                    a_A_log[0], a_D[0], a_norm_g[0], a_out_w[0], ln_g[0], ln_b[0], batch=batch, seq=seq)
    x2 = _dilated_block(x2, scale[1], shift[1], gate[1], kv_w, b_in_w[0], b_out_w[0], ln_g[1], ln_b[1],
                        batch=batch, seq=seq)
    return x2.reshape(batch, seq, d)
```

```python
import numpy as np
import jax
import jax.numpy as jnp
from jax import lax
from jax.experimental import pallas as pl
from jax.experimental.pallas import tpu as pltpu

F32 = jnp.float32
BF16 = jnp.bfloat16

D_MODEL = 2048
DEPTH = 2
LANES = 128
SUBLANES = 8

SSD_D_INNER = 2 * D_MODEL
SSD_HEAD_DIM = 64
SSD_N_HEADS = SSD_D_INNER // SSD_HEAD_DIM
SSD_N_GROUPS = 8
SSD_HEADS_PER_GROUP = SSD_N_HEADS // SSD_N_GROUPS
SSD_D_STATE = 128
SSD_CONV_W = 4
SSD_CHUNK = 256
SSD_BC_DIM = SSD_N_GROUPS * SSD_D_STATE
SSD_CONV_DIM = SSD_D_INNER + 2 * SSD_BC_DIM
SSD_GROUP_W = SSD_HEADS_PER_GROUP * SSD_HEAD_DIM

DIL_PATTERNS = ((128, 1), (512, 4), (2048, 16))
DIL_N_GROUPS = len(DIL_PATTERNS)
DIL_HEADS = 8
DIL_HEAD_DIM = 128
DIL_Q_WIDTH = DIL_N_GROUPS * DIL_HEADS * DIL_HEAD_DIM
DIL_OUT_WIDTH = DIL_HEADS * DIL_HEAD_DIM
DIL_BLOCK = 128
SHUFFLE_STRIDE = 4
ATTN_ROWS = 1024

DEEPNORM_ALPHA = (2 * DEPTH) ** 0.25
LN_EPS = 1e-5
RMS_EPS = 1e-5
LOG2E = 1.4426950408889634

VMEM_LIMIT = 56 * 1024 * 1024


def _params(sem):
    return pltpu.CompilerParams(dimension_semantics=sem, vmem_limit_bytes=VMEM_LIMIT)


def _silu(v):
    return v * (1.0 / (1.0 + jnp.exp(-v)))


def _split3(v):
    v1 = v.astype(BF16)
    r1 = v - v1.astype(F32)
    v2 = r1.astype(BF16)
    r2 = r1 - v2.astype(F32)
    v3 = r2.astype(BF16)
    return jnp.concatenate([v1, v2, v3], axis=1)


def _layer_norm(u, g, b):
    mu = jnp.mean(u, axis=-1, keepdims=True)
    d = u - mu
    var = jnp.mean(d * d, axis=-1, keepdims=True)
    return d * lax.rsqrt(var + LN_EPS) * g + b


def _adaln_kernel(c_ref, w_ref, b_ref, o_ref):
    s = _silu(c_ref[...]).astype(BF16)
    o_ref[...] = jnp.dot(s, w_ref[...].astype(BF16), preferred_element_type=F32) + b_ref[...]


def _adaln(c_pad, ada_w, ada_b):
    rows = c_pad.shape[0]
    n = 3 * D_MODEL
    tn = 768
    return pl.pallas_call(
        _adaln_kernel,
        grid=(DEPTH, n // tn),
        in_specs=[
            pl.BlockSpec((rows, D_MODEL), lambda l, j: (0, 0)),
            pl.BlockSpec((None, D_MODEL, tn), lambda l, j: (l, 0, j)),
            pl.BlockSpec((None, 1, tn), lambda l, j: (l, 0, j)),
        ],
        out_specs=pl.BlockSpec((None, rows, tn), lambda l, j: (l, 0, j)),
        out_shape=jax.ShapeDtypeStruct((DEPTH, rows, n), F32),
        compiler_params=_params(("parallel", "parallel")),
        name="adaln",
    )(c_pad, ada_w, ada_b.reshape(DEPTH, 1, n))


def _in_proj_kernel(x_ref, scale_ref, shift_ref, w_ref, wdt_ref, o_ref, dt_ref, h_ref):
    @pl.when(pl.program_id(1) == 0)
    def _():
        h = (x_ref[...] * (1.0 + scale_ref[...]) + shift_ref[...]).astype(BF16)
        h_ref[...] = h
        dt_ref[...] = jnp.dot(h, wdt_ref[...], preferred_element_type=F32)

    o_ref[...] = jnp.dot(h_ref[...], w_ref[...], preferred_element_type=F32)


def _in_proj(x2, scale, shift, w, wdt, *, n, seq, tm, tn):
    t, d = x2.shape
    per_batch = seq // tm
    return pl.pallas_call(
        _in_proj_kernel,
        grid=(t // tm, n // tn),
        in_specs=[
            pl.BlockSpec((tm, d), lambda i, j: (i, 0)),
            pl.BlockSpec((None, 1, d), lambda i, j: (i // per_batch, 0, 0)),
            pl.BlockSpec((None, 1, d), lambda i, j: (i // per_batch, 0, 0)),
            pl.BlockSpec((d, tn), lambda i, j: (0, j)),
            pl.BlockSpec((d, LANES), lambda i, j: (0, 0)),
        ],
        out_specs=[pl.BlockSpec((tm, tn), lambda i, j: (i, j)), pl.BlockSpec((tm, LANES), lambda i, j: (i, 0))],
        out_shape=[jax.ShapeDtypeStruct((t, n), F32), jax.ShapeDtypeStruct((t, LANES), F32)],
        scratch_shapes=[pltpu.VMEM((tm, d), BF16)],
        compiler_params=_params(("parallel", "arbitrary")),
        name="ssd_in_proj",
    )(x2, scale, shift, w, wdt)


def _conv_silu(raw, carry, w, b):
    wh = 0.5 * w
    acc = raw * wh[SSD_CONV_W - 1:SSD_CONV_W, :] + 0.5 * b
    row = lax.broadcasted_iota(jnp.int32, carry.shape, 0)
    for s in range(1, SSD_CONV_W):
        sh = pltpu.roll(raw, s, axis=0)
        prev = pltpu.roll(carry, s, axis=0)
        head = jnp.where(row < s, prev, sh[:SUBLANES])
        sh = jnp.concatenate([head, sh[SUBLANES:]], axis=0)
        acc = acc + sh * wh[SSD_CONV_W - 1 - s:SSD_CONV_W - s, :]
    return acc + acc * jnp.tanh(acc)


def _ssd_kernel(z_ref, x_ref, b_ref, c_ref, dt_ref, cw_ref, cb_ref, dtb_ref, alog_ref, dexp_ref, ng_ref,
                tril_ref, e_ref, sel_ref,
                y_ref,
                state_ref, carx_ref, carb_ref, carc_ref, v3_ref, ac3_ref, ybuf_ref):
    L = SSD_CHUNK
    gw = SSD_GROUP_W
    ns = SSD_D_STATE

    @pl.when(pl.program_id(1) == 0)
    def _():
        state_ref[...] = jnp.zeros(state_ref.shape, F32)
        carx_ref[...] = jnp.zeros(carx_ref.shape, F32)
        carb_ref[...] = jnp.zeros(carb_ref.shape, F32)
        carc_ref[...] = jnp.zeros(carc_ref.shape, F32)

    v = dt_ref[...] + dtb_ref[...]
    dt = jnp.maximum(v, 0.0) + jnp.log1p(jnp.exp(-jnp.abs(v)))
    a = dt * (-jnp.exp(alog_ref[...]))
    a3 = _split3(a)
    tril = tril_ref[...]
    acum = (jnp.dot(tril, a3[:, :LANES], preferred_element_type=F32)
            + jnp.dot(tril, a3[:, LANES:2 * LANES], preferred_element_type=F32)
            + jnp.dot(tril, a3[:, 2 * LANES:], preferred_element_type=F32))
    tail = jnp.exp(acum[L - 1:L, :] - acum)
    v3_ref[0:L, :] = _split3(dt)
    v3_ref[L:2 * L, :] = _split3(tail)
    v3_ref[2 * L:3 * L, :] = _split3(jnp.exp(acum))
    ac3_ref[...] = _split3(acum)

    li = lax.broadcasted_iota(jnp.int32, (L, L), 0)
    si = lax.broadcasted_iota(jnp.int32, (L, L), 1)
    causal = li >= si
    lane = lax.broadcasted_iota(jnp.int32, (L, LANES), 1)
    heads_per_tile = LANES // SSD_HEAD_DIM
    assert heads_per_tile == 2
    n_pairs = gw // LANES

    def group(g, carry):
        xs = pl.ds(pl.multiple_of(g * gw, gw), gw)
        bs = pl.ds(pl.multiple_of(g * ns, ns), ns)
        wb = pl.ds(pl.multiple_of(SSD_D_INNER + g * ns, ns), ns)
        wc = pl.ds(pl.multiple_of(SSD_D_INNER + SSD_BC_DIM + g * ns, ns), ns)

        ex = jnp.dot(v3_ref[...], e_ref[:, xs], preferred_element_type=F32)
        dt_e = ex[0:L]
        tail_e = ex[L:2 * L]
        eac_e = ex[2 * L:3 * L]
        acg = jnp.dot(ac3_ref[...], sel_ref[:, bs], preferred_element_type=F32) * LOG2E
        acg_t = acg.T

        b_raw = b_ref[:, bs]
        c_raw = c_ref[:, bs]
        bg = _conv_silu(b_raw, carb_ref[:, bs], cw_ref[:, wb], cb_ref[:, wb])
        cg = _conv_silu(c_raw, carc_ref[:, bs], cw_ref[:, wc], cb_ref[:, wc])
        carb_ref[:, bs] = b_raw[L - SUBLANES:, :]
        carc_ref[:, bs] = c_raw[L - SUBLANES:, :]
        cb16 = cg.astype(BF16)
        bb16 = bg.astype(BF16)
        cb = lax.dot_general(cb16, bb16, (((1,), (1,)), ((), ())), preferred_element_type=F32)
        cb = jnp.where(causal, cb, 0.0)

        state = state_ref[g]
        y_state = jnp.dot(cb16, state.astype(BF16), preferred_element_type=F32)

        ssq = jnp.zeros((L, LANES), F32)
        for pair in range(n_pairs):
            ps = slice(pair * LANES, (pair + 1) * LANES)
            cs = pl.ds(pl.multiple_of(g * gw + pair * LANES, LANES), LANES)
            x_raw = x_ref[:, cs]
            xg = _conv_silu(x_raw, carx_ref[:, cs], cw_ref[:, cs], cb_ref[:, cs])
            carx_ref[:, cs] = x_raw[L - SUBLANES:, :]
            xdt = xg * dt_e[:, ps]
            ms = []
            for sub in range(heads_per_tile):
                k = pair * heads_per_tile + sub
                seg = acg[:, k:k + 1] - acg_t[k:k + 1, :]
                ms.append((cb * jnp.exp2(jnp.minimum(seg, 0.0))).astype(BF16))
            both = jnp.dot(jnp.concatenate(ms, axis=0), xdt.astype(BF16), preferred_element_type=F32)
            y = (jnp.where(lane < SSD_HEAD_DIM, both[:L], both[L:]) + y_state[:, ps] * eac_e[:, ps]
                 + xg * dexp_ref[:, cs])
            xtail = (xdt * tail_e[:, ps]).astype(BF16)
            upd = lax.dot_general(bb16, xtail, (((0,), (0,)), ((), ())), preferred_element_type=F32)
            state_ref[g, :, ps] = state[:, ps] * eac_e[L - 1:L, ps] + upd
            hz = 0.5 * z_ref[:, cs]
            y = y * (hz + hz * jnp.tanh(hz))
            ssq = ssq + y * y
            ybuf_ref[:, ps] = y
        r = lax.rsqrt(jnp.sum(ssq, axis=-1, keepdims=True) * (1.0 / gw) + RMS_EPS)
        y_ref[:, xs] = (ybuf_ref[...] * r * ng_ref[:, xs]).astype(y_ref.dtype)
        return carry

    lax.fori_loop(0, SSD_N_GROUPS, group, 0)


def _ssd_constants():
    tril = np.tril(np.ones((SSD_CHUNK, SSD_CHUNK), np.float32))
    head_of_col = np.arange(SSD_D_INNER) // SSD_HEAD_DIM
    e = (np.arange(LANES)[:, None] == head_of_col[None, :]).astype(np.float32)
    col = np.arange(SSD_N_GROUPS * LANES)
    src = np.where(col % LANES < SSD_HEADS_PER_GROUP,
                   (col // LANES) * SSD_HEADS_PER_GROUP + col % LANES, -1)
    sel = (np.arange(LANES)[:, None] == src[None, :]).astype(np.float32)
    e3 = np.concatenate([e, e, e], axis=0)
    sel3 = np.concatenate([sel, sel, sel], axis=0)
    return jnp.asarray(tril, BF16), jnp.asarray(e3, BF16), jnp.asarray(sel3, BF16)


def _ssd_scan(zx, dt, conv_w, conv_b, dt_bias, a_log, d_skip, norm_g, *, batch, seq):
    t = zx.shape[0]
    L = SSD_CHUNK
    nc = seq // L
    G = SSD_N_GROUPS
    gw = SSD_GROUP_W
    ns = SSD_D_STATE
    tril, e3, sel3 = _ssd_constants()
    pad = LANES - SSD_N_HEADS
    dtb = jnp.pad(dt_bias, (0, pad)).reshape(1, LANES)
    alog = jnp.pad(a_log, (0, pad)).reshape(1, LANES)
    dexp = jnp.repeat(d_skip, SSD_HEAD_DIM).reshape(1, SSD_D_INNER)
    ng = norm_g.reshape(1, SSD_D_INNER)
    cb2 = conv_b.reshape(1, SSD_CONV_DIM)

    zw = SSD_D_INNER
    const = lambda b, c: (0, 0)
    in_specs = [
        pl.BlockSpec((L, zw), lambda b, c: (b * nc + c, 0)),
        pl.BlockSpec((L, zw), lambda b, c: (b * nc + c, 1)),
        pl.BlockSpec((L, SSD_BC_DIM), lambda b, c: (b * nc + c, 2 * zw // SSD_BC_DIM)),
        pl.BlockSpec((L, SSD_BC_DIM), lambda b, c: (b * nc + c, 2 * zw // SSD_BC_DIM + 1)),
        pl.BlockSpec((L, LANES), lambda b, c: (b * nc + c, 0)),
        pl.BlockSpec((SSD_CONV_W, SSD_CONV_DIM), const),
        pl.BlockSpec((1, SSD_CONV_DIM), const),
        pl.BlockSpec((1, LANES), const),
        pl.BlockSpec((1, LANES), const),
        pl.BlockSpec((1, zw), const),
        pl.BlockSpec((1, zw), const),
        pl.BlockSpec((L, L), const),
        pl.BlockSpec((3 * LANES, zw), const),
        pl.BlockSpec((3 * LANES, G * LANES), const),
    ]
    return pl.pallas_call(
        _ssd_kernel,
        grid=(batch, nc),
        in_specs=in_specs,
        out_specs=pl.BlockSpec((L, zw), lambda b, c: (b * nc + c, 0)),
        out_shape=jax.ShapeDtypeStruct((t, zw), BF16),
        scratch_shapes=[
            pltpu.VMEM((G, ns, gw), F32),
            pltpu.VMEM((SUBLANES, zw), F32),
            pltpu.VMEM((SUBLANES, SSD_BC_DIM), F32),
            pltpu.VMEM((SUBLANES, SSD_BC_DIM), F32),
            pltpu.VMEM((3 * L, 3 * LANES), BF16),
            pltpu.VMEM((L, 3 * LANES), BF16),
            pltpu.VMEM((L, gw), F32),
        ],
        compiler_params=_params(("parallel", "arbitrary")),
        name="ssd_scan",
    )(zx, zx, zx, zx, dt, conv_w, cb2, dtb, alog, dexp, ng, tril, e3, sel3)


def _proj_ln_kernel(y_ref, w_ref, x_ref, gate_ref, g_ref, b_ref, o_ref):
    acc = jnp.dot(y_ref[...], w_ref[...], preferred_element_type=F32)
    u = DEEPNORM_ALPHA * x_ref[...] + (1.0 + gate_ref[...]) * acc
    o_ref[...] = _layer_norm(u, g_ref[...], b_ref[...])


def _proj_ln(y, w, x2, gate, ln_g, ln_b, *, seq, tm):
    t, kdim = y.shape
    d = w.shape[1]
    per_batch = seq // tm
    return pl.pallas_call(
        _proj_ln_kernel,
        grid=(t // tm,),
        in_specs=[
            pl.BlockSpec((tm, kdim), lambda i: (i, 0)),
            pl.BlockSpec((kdim, d), lambda i: (0, 0), pipeline_mode=pl.Buffered(1)),
            pl.BlockSpec((tm, d), lambda i: (i, 0)),
            pl.BlockSpec((None, 1, d), lambda i: (i // per_batch, 0, 0)),
            pl.BlockSpec((1, d), lambda i: (0, 0)),
            pl.BlockSpec((1, d), lambda i: (0, 0)),
        ],
        out_specs=pl.BlockSpec((tm, d), lambda i: (i, 0)),
        out_shape=jax.ShapeDtypeStruct((t, d), F32),
        compiler_params=_params(("parallel",)),
        name="out_proj_ln",
    )(y, w, x2, gate, ln_g.reshape(1, d), ln_b.reshape(1, d))


def _b_proj_kernel(x_ref, scale_ref, shift_ref, w_ref, o0_ref, o1_ref, o2_ref, z_ref, h_ref, res_ref, tmp_ref):
    tm = x_ref.shape[0]
    n_qkv = 3 * DIL_N_GROUPS
    ow = DIL_OUT_WIDTH
    block_tiles = ow // LANES
    tiles = 2
    n_chunks = w_ref.shape[1] // (tiles * LANES)
    step_blocks = w_ref.shape[1] // ow
    out_refs = (o0_ref, o1_ref, o2_ref)

    @pl.when(pl.program_id(1) == 0)
    def _():
        x = x_ref[...]
        h_ref[0] = (x * (1.0 + scale_ref[...]) + shift_ref[...]).astype(BF16)
        h_ref[1] = x.astype(BF16)

    def emit_tile(block, gt):
        cols = slice((gt % block_tiles) * LANES, (gt % block_tiles + 1) * LANES)
        if block == n_qkv:
            z_ref[:, cols] = res_ref[gt]
            return
        out_ref = out_refs[block % DIL_N_GROUPS]
        dilation = DIL_PATTERNS[block % DIL_N_GROUPS][1]
        rows = tm // dilation
        if dilation <= SHUFFLE_STRIDE:
            for r in range(dilation):
                out_ref[r, :, cols] = res_ref[gt, pl.ds(r, rows, stride=dilation), :].astype(BF16)
        else:
            assert dilation == SHUFFLE_STRIDE * SHUFFLE_STRIDE
            tmp = tmp_ref.at[gt % 2]
            for r_lo in range(SHUFFLE_STRIDE):
                tmp[r_lo] = res_ref[gt, pl.ds(r_lo, tm // SHUFFLE_STRIDE, stride=SHUFFLE_STRIDE), :]
            for r_lo in range(SHUFFLE_STRIDE):
                for r_hi in range(SHUFFLE_STRIDE):
                    out_ref[r_lo + SHUFFLE_STRIDE * r_hi, :, cols] = (
                        tmp[r_lo, pl.ds(r_hi, rows, stride=SHUFFLE_STRIDE), :].astype(BF16))

    def step(jj):
        def block_of(gt):
            return jj * step_blocks + gt // block_tiles

        def matmul_chunk(c):
            block = block_of(c * tiles)
            plain = 1 if DIL_N_GROUPS <= block < n_qkv else 0
            cols = slice(c * tiles * LANES, (c + 1) * tiles * LANES)
            acc = jnp.dot(h_ref[plain], w_ref[:, cols], preferred_element_type=F32)
            for t in range(tiles):
                res_ref[c * tiles + t] = acc[:, t * LANES:(t + 1) * LANES]

        matmul_chunk(0)
        for c in range(1, n_chunks + 1):
            if c < n_chunks:
                matmul_chunk(c)
            for gt in range((c - 1) * tiles, c * tiles):
                emit_tile(block_of(gt), gt)

    for jj in range((n_qkv + 1) // step_blocks):
        pl.when(pl.program_id(1) == jj)(lambda jj=jj: step(jj))


def _b_proj(x2, scale, shift, w, *, batch, seq, tm):
    t, d = x2.shape
    ow = DIL_OUT_WIDTH
    per_batch = seq // tm
    step_blocks = 2
    assert DIL_N_GROUPS == 3 and w.shape[1] == (3 * DIL_N_GROUPS + 1) * ow
    part_at_step = (lambda j: jnp.minimum((j + 1) // 2, 2),
                    lambda j: jnp.clip(j - 1, 0, 2),
                    lambda j: j // 2)
    out_specs, out_shape = [], []
    for (_, dilation), part in zip(DIL_PATTERNS, part_at_step):
        out_specs.append(pl.BlockSpec(
            (None, dilation, tm // dilation, ow),
            lambda i, j, part=part: (i // per_batch, 0, i % per_batch, part(j))))
        out_shape.append(jax.ShapeDtypeStruct((batch, dilation, seq // dilation, 3 * ow), BF16))
    out_specs.append(pl.BlockSpec((tm, ow), lambda i, j: (i, 0)))
    out_shape.append(jax.ShapeDtypeStruct((t, ow), F32))
    return pl.pallas_call(
        _b_proj_kernel,
        grid=(t // tm, w.shape[1] // (step_blocks * ow)),
        in_specs=[
            pl.BlockSpec((tm, d), lambda i, j: (i, 0)),
            pl.BlockSpec((None, 1, d), lambda i, j: (i // per_batch, 0, 0)),
            pl.BlockSpec((None, 1, d), lambda i, j: (i // per_batch, 0, 0)),
            pl.BlockSpec((d, step_blocks * ow), lambda i, j: (0, j)),
        ],
        out_specs=out_specs,
        out_shape=out_shape,
        scratch_shapes=[pltpu.VMEM((2, tm, d), BF16), pltpu.VMEM((step_blocks * ow // LANES, tm, LANES), F32),
                        pltpu.VMEM((2, SHUFFLE_STRIDE, tm // SHUFFLE_STRIDE, LANES), F32)],
        compiler_params=_params(("parallel", "arbitrary")),
        name="dilated_in_proj",
    )(x2, scale, shift, w)


def _attn_bias(group):
    _, dilation = DIL_PATTERNS[group]
    n_all = DIL_N_GROUPS * DIL_HEADS
    slopes = 2.0 ** (-8.0 * np.arange(1, n_all + 1) / n_all)
    slopes = slopes.reshape(DIL_N_GROUPS, DIL_HEADS)[group].astype(np.float32)
    qi = np.arange(DIL_BLOCK)[:, None]
    kj = np.arange(2 * DIL_BLOCK)[None, :]
    delta = qi + DIL_BLOCK - kj
    valid = (delta >= 0) & (delta <= DIL_BLOCK)
    alibi = -slopes[:, None, None] * (delta * dilation).astype(np.float32)[None]
    return jnp.asarray(np.where(valid[None], alibi * np.float32(LOG2E), -np.inf).astype(np.float32))


def _attn_kernel(q_ref, k_ref, v_ref, kp_ref, vp_ref, bias_ref, o_ref, st_ref):
    first = pl.program_id(2) == 0
    blk = DIL_BLOCK
    n_sub, tq = q_ref.shape[0], q_ref.shape[1]
    scale = DIL_HEAD_DIM ** -0.5 * LOG2E
    dn = (((1,), (1,)), ((), ()))
    lane = lax.broadcasted_iota(jnp.int32, (blk, LANES), 1)
    kcol = lax.broadcasted_iota(jnp.int32, (blk, 2 * blk), 1)
    no_prev = first & (kcol < blk)
    for r in range(n_sub):
        for sb in range(tq // blk):
            rows = slice(sb * blk, (sb + 1) * blk)
            keys = slice((sb - 1) * blk, (sb + 1) * blk)
            m_tile = jnp.zeros((blk, LANES), F32)
            den_tile = jnp.ones((blk, LANES), F32)
            ss = []
            for h in range(DIL_HEADS):
                cols = slice(h * DIL_HEAD_DIM, (h + 1) * DIL_HEAD_DIM)
                q = q_ref[r, rows, cols]
                if sb == 0:
                    s = jnp.concatenate(
                        [lax.dot_general(q, kp_ref[r, :, cols], dn, preferred_element_type=F32),
                         lax.dot_general(q, k_ref[r, rows, cols], dn, preferred_element_type=F32)], axis=1)
                else:
                    s = lax.dot_general(q, k_ref[r, keys, cols], dn, preferred_element_type=F32)
                ss.append(s)
            ps = []
            for h in range(DIL_HEADS):
                bias = bias_ref[h]
                if sb == 0:
                    bias = jnp.where(no_prev, -jnp.inf, bias)
                s = ss[h] * scale + bias
                m = jnp.max(s, axis=-1, keepdims=True)
                p = jnp.exp2(s - m)
                den = jnp.sum(p, axis=-1, keepdims=True)
                ps.append(p.astype(BF16))
                m_tile = jnp.where(lane == h, m, m_tile)
                den_tile = jnp.where(lane == h, den, den_tile)
            for h in range(DIL_HEADS):
                cols = slice(h * DIL_HEAD_DIM, (h + 1) * DIL_HEAD_DIM)
                p16 = ps[h]
                if sb == 0:
                    o = (jnp.dot(p16[:, :blk], vp_ref[r, :, cols], preferred_element_type=F32)
                         + jnp.dot(p16[:, blk:], v_ref[r, rows, cols], preferred_element_type=F32))
                else:
                    o = jnp.dot(p16, v_ref[r, keys, cols], preferred_element_type=F32)
                o_ref[r, rows, cols] = o
            st_ref[r, rows, :LANES] = (m_tile + jnp.log2(den_tile)) * (1.0 / LOG2E)
            st_ref[r, rows, LANES:] = 1.0 / den_tile


def _dilated_attention(qkv, group, *, n_sub, tq):
    batch, dilation, m, _ = qkv.shape
    assert DIL_PATTERNS[group][0] // dilation == DIL_BLOCK and m % tq == 0 and dilation % n_sub == 0
    w = DIL_OUT_WIDTH
    blk = DIL_BLOCK
    sub = tq // blk

    def prev(i):
        return jnp.maximum(i * sub - 1, 0)

    return pl.pallas_call(
        _attn_kernel,
        grid=(batch, dilation // n_sub, m // tq),
        in_specs=[
            pl.BlockSpec((None, n_sub, tq, w), lambda b, r, i: (b, r, i, 0)),
            pl.BlockSpec((None, n_sub, tq, w), lambda b, r, i: (b, r, i, 1)),
            pl.BlockSpec((None, n_sub, tq, w), lambda b, r, i: (b, r, i, 2)),
            pl.BlockSpec((None, n_sub, blk, w), lambda b, r, i: (b, r, prev(i), 1)),
            pl.BlockSpec((None, n_sub, blk, w), lambda b, r, i: (b, r, prev(i), 2)),
            pl.BlockSpec((DIL_HEADS, blk, 2 * blk), lambda b, r, i: (0, 0, 0)),
        ],
        out_specs=[
            pl.BlockSpec((None, n_sub, tq, w), lambda b, r, i: (b, r, i, 0)),
            pl.BlockSpec((None, n_sub, tq, 2 * LANES), lambda b, r, i: (b, r, i, 0)),
        ],
        out_shape=[
            jax.ShapeDtypeStruct((batch, dilation, m, w), F32),
            jax.ShapeDtypeStruct((batch, dilation, m, 2 * LANES), F32),
        ],
        compiler_params=_params(("parallel", "parallel", "arbitrary")),
        name=f"dilated_attn_{group}",
    )(qkv, qkv, qkv, qkv, qkv, _attn_bias(group))


def _merge_ln_kernel(o0_ref, o1_ref, o2_ref, s0_ref, s1_ref, s2_ref, z_ref, w_ref, x_ref,
                     gate_ref, g_ref, b_ref, out_ref, on_ref, sn_ref, tmp_ref):
    tm = x_ref.shape[0]
    assert DIL_PATTERNS[0][1] == 1
    def scatter(dst, src_ref, cols, dilation, tmp):
        rows = tm // dilation
        if dilation <= SHUFFLE_STRIDE:
            for r in range(dilation):
                dst[pl.ds(r, rows, stride=dilation), :] = src_ref[r, :, cols]
        else:
            assert dilation == SHUFFLE_STRIDE * SHUFFLE_STRIDE
            for r_lo in range(SHUFFLE_STRIDE):
                for r_hi in range(SHUFFLE_STRIDE):
                    tmp[r_lo, pl.ds(r_hi, rows, stride=SHUFFLE_STRIDE), :] = (
                        src_ref[r_lo + SHUFFLE_STRIDE * r_hi, :, cols])
            for r_lo in range(SHUFFLE_STRIDE):
                dst[pl.ds(r_lo, tm // SHUFFLE_STRIDE, stride=SHUFFLE_STRIDE), :] = tmp[r_lo]

    n_tmp = tmp_ref.shape[0]
    for g, (o_ref, s_ref) in ((1, (o1_ref, s1_ref)), (2, (o2_ref, s2_ref))):
        dilation = DIL_PATTERNS[g][1]
        for half in range(2):
            scatter(sn_ref.at[g - 1, half], s_ref, slice(half * LANES, (half + 1) * LANES), dilation,
                    tmp_ref.at[half % n_tmp])
        for h in range(DIL_HEADS):
            scatter(on_ref.at[g - 1, h], o_ref, slice(h * DIL_HEAD_DIM, (h + 1) * DIL_HEAD_DIM), dilation,
                    tmp_ref.at[h % n_tmp])
    l0, l1, l2 = s0_ref[0, :, :LANES], sn_ref[0, 0], sn_ref[1, 0]
    mx = jnp.maximum(jnp.maximum(l0, l1), l2)
    e0, e1, e2 = jnp.exp(l0 - mx), jnp.exp(l1 - mx), jnp.exp(l2 - mx)
    inv = 1.0 / (e0 + e1 + e2)
    c0 = e0 * inv * s0_ref[0, :, LANES:]
    c1 = e1 * inv * sn_ref[0, 1]
    c2 = e2 * inv * sn_ref[1, 1]
    pieces = []
    for h in range(DIL_HEADS):
        pieces.append(o0_ref[0, :, h * DIL_HEAD_DIM:(h + 1) * DIL_HEAD_DIM] * c0[:, h:h + 1]
                      + on_ref[0, h] * c1[:, h:h + 1]
                      + on_ref[1, h] * c2[:, h:h + 1])
    hz = 0.5 * z_ref[...]
    o = jnp.concatenate(pieces, axis=1) * (hz + hz * jnp.tanh(hz))
    y = jnp.dot(o.astype(BF16), w_ref[...], preferred_element_type=F32)
    u = DEEPNORM_ALPHA * x_ref[...] + (1.0 + gate_ref[...]) * y
    out_ref[...] = _layer_norm(u, g_ref[...], b_ref[...])


def _merge_ln(os_, stats, z, w, x2, gate, ln_g, ln_b, *, seq, tm):
    t, d = x2.shape
    ow = DIL_OUT_WIDTH
    per_batch = seq // tm
    row = lambda i: (i, 0)

    def sub_major(width):
        return [pl.BlockSpec((None, dilation, tm // dilation, width),
                             lambda i: (i // per_batch, 0, i % per_batch, 0))
                for _, dilation in DIL_PATTERNS]

    return pl.pallas_call(
        _merge_ln_kernel,
        grid=(t // tm,),
        in_specs=sub_major(ow) + sub_major(2 * LANES) + [
            pl.BlockSpec((tm, ow), row),
            pl.BlockSpec((ow, d), lambda i: (0, 0)),
            pl.BlockSpec((tm, d), row),
            pl.BlockSpec((None, 1, d), lambda i: (i // per_batch, 0, 0)),
            pl.BlockSpec((1, d), lambda i: (0, 0)),
            pl.BlockSpec((1, d), lambda i: (0, 0)),
        ],
        out_specs=pl.BlockSpec((tm, d), row),
        out_shape=jax.ShapeDtypeStruct((t, d), F32),
        scratch_shapes=[pltpu.VMEM((DIL_N_GROUPS - 1, DIL_HEADS, tm, LANES), F32),
                        pltpu.VMEM((DIL_N_GROUPS - 1, 2, tm, LANES), F32),
                        pltpu.VMEM((2, SHUFFLE_STRIDE, tm // SHUFFLE_STRIDE, LANES), F32)],
        compiler_params=_params(("parallel",)),
        name="merge_out_proj_ln",
    )(*os_, *stats, z, w, x2, gate, ln_g.reshape(1, d), ln_b.reshape(1, d))


def _ssd_block(x2, scale, shift, gate, in_w, conv_w, conv_b, dt_bias, a_log, d_skip, norm_g, out_w, ln_g, ln_b,
               *, batch, seq):
    w16 = in_w.astype(BF16)
    dt_w = jnp.pad(w16[:, SSD_D_INNER + SSD_CONV_DIM:], ((0, 0), (0, LANES - SSD_N_HEADS)))
    zx, dt = _in_proj(x2, scale, shift, w16, dt_w, n=SSD_D_INNER + SSD_CONV_DIM, seq=seq, tm=1024, tn=1024)
    y = _ssd_scan(zx, dt, conv_w, conv_b, dt_bias, a_log, d_skip, norm_g, batch=batch, seq=seq)
    return _proj_ln(y, out_w.astype(BF16), x2, gate, ln_g, ln_b, seq=seq, tm=512)


def _dilated_block(x2, scale, shift, gate, kv_w, in_w, out_w, ln_g, ln_b, *, batch, seq):
    w = jnp.concatenate([in_w[:, :DIL_Q_WIDTH], kv_w, in_w[:, DIL_Q_WIDTH:]], axis=1).astype(BF16)
    *qkvs, z = _b_proj(x2, scale, shift, w, batch=batch, seq=seq, tm=512)
    os_, stats = [], []
    for group in range(DIL_N_GROUPS):
        tq = min(ATTN_ROWS, seq // DIL_PATTERNS[group][1])
        o, st = _dilated_attention(qkvs[group], group, n_sub=ATTN_ROWS // tq, tq=tq)
        os_.append(o)
        stats.append(st)
    return _merge_ln(os_, stats, z, out_w.astype(BF16), x2, gate, ln_g, ln_b, seq=seq, tm=512)


def kernel(x, c, ada_w, ada_b, ln_g, ln_b, a_in_w, a_conv_w, a_conv_b, a_dt_bias, a_A_log, a_D,
           a_norm_g, a_out_w, kv_w, b_in_w, b_out_w):
    batch, seq, d = x.shape
    x2 = x.reshape(batch * seq, d)

    c_pad = jnp.pad(c, ((0, 2 * SUBLANES - batch), (0, 0)))
    mod = _adaln(c_pad, ada_w, ada_b)[:, :batch]
    shift = mod[:, :, None, 0:d]
    scale = mod[:, :, None, d:2 * d]
    gate = mod[:, :, None, 2 * d:3 * d]

    x2 = _ssd_block(x2, scale[0], shift[0], gate[0], a_in_w[0], a_conv_w[0], a_conv_b[0], a_dt_bias[0],
                    a_A_log[0], a_D[0], a_norm_g[0], a_out_w[0], ln_g[0], ln_b[0], batch=batch, seq=seq)
    x2 = _dilated_block(x2, scale[1], shift[1], gate[1], kv_w, b_in_w[0], b_out_w[0], ln_g[1], ln_b[1],
                        batch=batch, seq=seq)
    return x2.reshape(batch, seq, d)
```

```python
import numpy as np
import jax
import jax.numpy as jnp
from jax import lax
from jax.experimental import pallas as pl
from jax.experimental.pallas import tpu as pltpu

F32 = jnp.float32
BF16 = jnp.bfloat16

D_MODEL = 2048
DEPTH = 2
LANES = 128
SUBLANES = 8

SSD_D_INNER = 2 * D_MODEL
SSD_HEAD_DIM = 64
SSD_N_HEADS = SSD_D_INNER // SSD_HEAD_DIM
SSD_N_GROUPS = 8
SSD_HEADS_PER_GROUP = SSD_N_HEADS // SSD_N_GROUPS
SSD_D_STATE = 128
SSD_CONV_W = 4
SSD_CHUNK = 256
SSD_BC_DIM = SSD_N_GROUPS * SSD_D_STATE
SSD_CONV_DIM = SSD_D_INNER + 2 * SSD_BC_DIM
SSD_GROUP_W = SSD_HEADS_PER_GROUP * SSD_HEAD_DIM

DIL_PATTERNS = ((128, 1), (512, 4), (2048, 16))
DIL_N_GROUPS = len(DIL_PATTERNS)
DIL_HEADS = 8
DIL_HEAD_DIM = 128
DIL_Q_WIDTH = DIL_N_GROUPS * DIL_HEADS * DIL_HEAD_DIM
DIL_OUT_WIDTH = DIL_HEADS * DIL_HEAD_DIM
DIL_BLOCK = 128
SHUFFLE_STRIDE = 4
ATTN_ROWS = 1024

DEEPNORM_ALPHA = (2 * DEPTH) ** 0.25
LN_EPS = 1e-5
RMS_EPS = 1e-5
LOG2E = 1.4426950408889634

VMEM_LIMIT = 56 * 1024 * 1024


def _params(sem):
    return pltpu.CompilerParams(dimension_semantics=sem, vmem_limit_bytes=VMEM_LIMIT)


def _silu(v):
    return v * (1.0 / (1.0 + jnp.exp(-v)))


def _split3(v):
    v1 = v.astype(BF16)
    r1 = v - v1.astype(F32)
    v2 = r1.astype(BF16)
    r2 = r1 - v2.astype(F32)
    v3 = r2.astype(BF16)
    return jnp.concatenate([v1, v2, v3], axis=1)


def _layer_norm(u, g, b):
    mu = jnp.mean(u, axis=-1, keepdims=True)
    d = u - mu
    var = jnp.mean(d * d, axis=-1, keepdims=True)
    return d * lax.rsqrt(var + LN_EPS) * g + b


def _adaln_kernel(c_ref, w_ref, b_ref, o_ref):
    s = _silu(c_ref[...]).astype(BF16)
    o_ref[...] = jnp.dot(s, w_ref[...].astype(BF16), preferred_element_type=F32) + b_ref[...]


def _adaln(c_pad, ada_w, ada_b):
    rows = c_pad.shape[0]
    n = 3 * D_MODEL
    tn = 768
    return pl.pallas_call(
        _adaln_kernel,
        grid=(DEPTH, n // tn),
        in_specs=[
            pl.BlockSpec((rows, D_MODEL), lambda l, j: (0, 0)),
            pl.BlockSpec((None, D_MODEL, tn), lambda l, j: (l, 0, j)),
            pl.BlockSpec((None, 1, tn), lambda l, j: (l, 0, j)),
        ],
        out_specs=pl.BlockSpec((None, rows, tn), lambda l, j: (l, 0, j)),
        out_shape=jax.ShapeDtypeStruct((DEPTH, rows, n), F32),
        compiler_params=_params(("parallel", "parallel")),
        name="adaln",
    )(c_pad, ada_w, ada_b.reshape(DEPTH, 1, n))


def _in_proj_kernel(x_ref, scale_ref, shift_ref, w_ref, wdt_ref, o_ref, dt_ref, h_ref):
    @pl.when(pl.program_id(1) == 0)
    def _():
        h = (x_ref[...] * (1.0 + scale_ref[...]) + shift_ref[...]).astype(BF16)
        h_ref[...] = h
        dt_ref[...] = jnp.dot(h, wdt_ref[...], preferred_element_type=F32)

    o_ref[...] = jnp.dot(h_ref[...], w_ref[...], preferred_element_type=F32)


def _in_proj(x2, scale, shift, w, wdt, *, n, seq, tm, tn):
    t, d = x2.shape
    per_batch = seq // tm
    return pl.pallas_call(
        _in_proj_kernel,
        grid=(t // tm, n // tn),
        in_specs=[
            pl.BlockSpec((tm, d), lambda i, j: (i, 0)),
            pl.BlockSpec((None, 1, d), lambda i, j: (i // per_batch, 0, 0)),
            pl.BlockSpec((None, 1, d), lambda i, j: (i // per_batch, 0, 0)),
            pl.BlockSpec((d, tn), lambda i, j: (0, j)),
            pl.BlockSpec((d, LANES), lambda i, j: (0, 0)),
        ],
        out_specs=[pl.BlockSpec((tm, tn), lambda i, j: (i, j)), pl.BlockSpec((tm, LANES), lambda i, j: (i, 0))],
        out_shape=[jax.ShapeDtypeStruct((t, n), F32), jax.ShapeDtypeStruct((t, LANES), F32)],
        scratch_shapes=[pltpu.VMEM((tm, d), BF16)],
        compiler_params=_params(("parallel", "arbitrary")),
        name="ssd_in_proj",
    )(x2, scale, shift, w, wdt)


def _conv_silu(raw, carry, w, b):
    assert SSD_CONV_W == 4
    wh = 0.5 * w
    row = lax.broadcasted_iota(jnp.int32, carry.shape, 0)

    def delayed(cur, tail, s):
        sh = pltpu.roll(cur, s, axis=0)
        prev = pltpu.roll(tail, s, axis=0)
        head = jnp.where(row < s, prev, sh[:SUBLANES])
        return jnp.concatenate([head, sh[SUBLANES:]], axis=0)

    x1 = delayed(raw, carry, 1)
    near = raw * wh[3:4, :] + x1 * wh[2:3, :] + 0.5 * b
    far = raw * wh[1:2, :] + x1 * wh[0:1, :]
    far_tail = carry * wh[1:2, :] + pltpu.roll(carry, 1, axis=0) * wh[0:1, :]
    acc = near + delayed(far, far_tail, 2)
    return acc + acc * jnp.tanh(acc)


def _ssd_kernel(z_ref, x_ref, b_ref, c_ref, dt_ref, cw_ref, cb_ref, dtb_ref, alog_ref, dexp_ref, ng_ref,
                tril_ref, e_ref, sel_ref,
                y_ref,
                state_ref, carx_ref, carb_ref, carc_ref, v3_ref, ac3_ref, ybuf_ref):
    L = SSD_CHUNK
    gw = SSD_GROUP_W
    ns = SSD_D_STATE

    @pl.when(pl.program_id(1) == 0)
    def _():
        state_ref[...] = jnp.zeros(state_ref.shape, F32)
        carx_ref[...] = jnp.zeros(carx_ref.shape, F32)
        carb_ref[...] = jnp.zeros(carb_ref.shape, F32)
        carc_ref[...] = jnp.zeros(carc_ref.shape, F32)

    v = dt_ref[...] + dtb_ref[...]
    dt = jnp.maximum(v, 0.0) + jnp.log1p(jnp.exp(-jnp.abs(v)))
    a = dt * (-jnp.exp(alog_ref[...]))
    a3 = _split3(a)
    tril = tril_ref[...]
    acum = (jnp.dot(tril, a3[:, :LANES], preferred_element_type=F32)
            + jnp.dot(tril, a3[:, LANES:2 * LANES], preferred_element_type=F32)
            + jnp.dot(tril, a3[:, 2 * LANES:], preferred_element_type=F32))
    tail = jnp.exp(acum[L - 1:L, :] - acum)
    v3_ref[0:L, :] = _split3(dt)
    v3_ref[L:2 * L, :] = _split3(tail)
    v3_ref[2 * L:3 * L, :] = _split3(jnp.exp(acum))
    ac3_ref[...] = _split3(acum)

    li = lax.broadcasted_iota(jnp.int32, (L, L), 0)
    si = lax.broadcasted_iota(jnp.int32, (L, L), 1)
    causal = li >= si
    lane = lax.broadcasted_iota(jnp.int32, (L, LANES), 1)
    heads_per_tile = LANES // SSD_HEAD_DIM
    assert heads_per_tile == 2
    n_pairs = gw // LANES

    def group(g, carry):
        xs = pl.ds(pl.multiple_of(g * gw, gw), gw)
        bs = pl.ds(pl.multiple_of(g * ns, ns), ns)
        wb = pl.ds(pl.multiple_of(SSD_D_INNER + g * ns, ns), ns)
        wc = pl.ds(pl.multiple_of(SSD_D_INNER + SSD_BC_DIM + g * ns, ns), ns)

        ex = jnp.dot(v3_ref[...], e_ref[:, xs], preferred_element_type=F32)
        dt_e = ex[0:L]
        tail_e = ex[L:2 * L]
        eac_e = ex[2 * L:3 * L]
        acg = jnp.dot(ac3_ref[...], sel_ref[:, bs], preferred_element_type=F32) * LOG2E
        acg_t = acg.T

        b_raw = b_ref[:, bs]
        c_raw = c_ref[:, bs]
        bg = _conv_silu(b_raw, carb_ref[:, bs], cw_ref[:, wb], cb_ref[:, wb])
        cg = _conv_silu(c_raw, carc_ref[:, bs], cw_ref[:, wc], cb_ref[:, wc])
        carb_ref[:, bs] = b_raw[L - SUBLANES:, :]
        carc_ref[:, bs] = c_raw[L - SUBLANES:, :]
        cb16 = cg.astype(BF16)
        bb16 = bg.astype(BF16)
        cb = lax.dot_general(cb16, bb16, (((1,), (1,)), ((), ())), preferred_element_type=F32)
        cb = jnp.where(causal, cb, 0.0)

        state = state_ref[g]
        y_state = jnp.dot(cb16, state.astype(BF16), preferred_element_type=F32)

        ssq = jnp.zeros((L, LANES), F32)
        for pair in range(n_pairs):
            ps = slice(pair * LANES, (pair + 1) * LANES)
            cs = pl.ds(pl.multiple_of(g * gw + pair * LANES, LANES), LANES)
            x_raw = x_ref[:, cs]
            xg = _conv_silu(x_raw, carx_ref[:, cs], cw_ref[:, cs], cb_ref[:, cs])
            carx_ref[:, cs] = x_raw[L - SUBLANES:, :]
            xdt = xg * dt_e[:, ps]
            ms = []
            for sub in range(heads_per_tile):
                k = pair * heads_per_tile + sub
                seg = acg[:, k:k + 1] - acg_t[k:k + 1, :]
                ms.append((cb * jnp.exp2(jnp.minimum(seg, 0.0))).astype(BF16))
            both = jnp.dot(jnp.concatenate(ms, axis=0), xdt.astype(BF16), preferred_element_type=F32)
            y = (jnp.where(lane < SSD_HEAD_DIM, both[:L], both[L:]) + y_state[:, ps] * eac_e[:, ps]
                 + xg * dexp_ref[:, cs])
            xtail = (xdt * tail_e[:, ps]).astype(BF16)
            upd = lax.dot_general(bb16, xtail, (((0,), (0,)), ((), ())), preferred_element_type=F32)
            state_ref[g, :, ps] = state[:, ps] * eac_e[L - 1:L, ps] + upd
            hz = 0.5 * z_ref[:, cs]
            y = y * (hz + hz * jnp.tanh(hz))
            ssq = ssq + y * y
            ybuf_ref[:, ps] = y
        r = lax.rsqrt(jnp.sum(ssq, axis=-1, keepdims=True) * (1.0 / gw) + RMS_EPS)
        y_ref[:, xs] = (ybuf_ref[...] * r * ng_ref[:, xs]).astype(y_ref.dtype)
        return carry

    lax.fori_loop(0, SSD_N_GROUPS, group, 0)


def _ssd_constants():
    tril = np.tril(np.ones((SSD_CHUNK, SSD_CHUNK), np.float32))
    head_of_col = np.arange(SSD_D_INNER) // SSD_HEAD_DIM
    e = (np.arange(LANES)[:, None] == head_of_col[None, :]).astype(np.float32)
    col = np.arange(SSD_N_GROUPS * LANES)
    src = np.where(col % LANES < SSD_HEADS_PER_GROUP,
                   (col // LANES) * SSD_HEADS_PER_GROUP + col % LANES, -1)
    sel = (np.arange(LANES)[:, None] == src[None, :]).astype(np.float32)
    e3 = np.concatenate([e, e, e], axis=0)
    sel3 = np.concatenate([sel, sel, sel], axis=0)
    return jnp.asarray(tril, BF16), jnp.asarray(e3, BF16), jnp.asarray(sel3, BF16)


def _ssd_scan(zx, dt, conv_w, conv_b, dt_bias, a_log, d_skip, norm_g, *, batch, seq):
    t = zx.shape[0]
    L = SSD_CHUNK
    nc = seq // L
    G = SSD_N_GROUPS
    gw = SSD_GROUP_W
    ns = SSD_D_STATE
    tril, e3, sel3 = _ssd_constants()
    pad = LANES - SSD_N_HEADS
    dtb = jnp.pad(dt_bias, (0, pad)).reshape(1, LANES)
    alog = jnp.pad(a_log, (0, pad)).reshape(1, LANES)
    dexp = jnp.repeat(d_skip, SSD_HEAD_DIM).reshape(1, SSD_D_INNER)
    ng = norm_g.reshape(1, SSD_D_INNER)
    cb2 = conv_b.reshape(1, SSD_CONV_DIM)

    zw = SSD_D_INNER
    const = lambda b, c: (0, 0)
    in_specs = [
        pl.BlockSpec((L, zw), lambda b, c: (b * nc + c, 0)),
        pl.BlockSpec((L, zw), lambda b, c: (b * nc + c, 1)),
        pl.BlockSpec((L, SSD_BC_DIM), lambda b, c: (b * nc + c, 2 * zw // SSD_BC_DIM)),
        pl.BlockSpec((L, SSD_BC_DIM), lambda b, c: (b * nc + c, 2 * zw // SSD_BC_DIM + 1)),
        pl.BlockSpec((L, LANES), lambda b, c: (b * nc + c, 0)),
        pl.BlockSpec((SSD_CONV_W, SSD_CONV_DIM), const),
        pl.BlockSpec((1, SSD_CONV_DIM), const),
        pl.BlockSpec((1, LANES), const),
        pl.BlockSpec((1, LANES), const),
        pl.BlockSpec((1, zw), const),
        pl.BlockSpec((1, zw), const),
        pl.BlockSpec((L, L), const),
        pl.BlockSpec((3 * LANES, zw), const),
        pl.BlockSpec((3 * LANES, G * LANES), const),
    ]
    return pl.pallas_call(
        _ssd_kernel,
        grid=(batch, nc),
        in_specs=in_specs,
        out_specs=pl.BlockSpec((L, zw), lambda b, c: (b * nc + c, 0)),
        out_shape=jax.ShapeDtypeStruct((t, zw), BF16),
        scratch_shapes=[
            pltpu.VMEM((G, ns, gw), F32),
            pltpu.VMEM((SUBLANES, zw), F32),
            pltpu.VMEM((SUBLANES, SSD_BC_DIM), F32),
            pltpu.VMEM((SUBLANES, SSD_BC_DIM), F32),
            pltpu.VMEM((3 * L, 3 * LANES), BF16),
            pltpu.VMEM((L, 3 * LANES), BF16),
            pltpu.VMEM((L, gw), F32),
        ],
        compiler_params=_params(("parallel", "arbitrary")),
        name="ssd_scan",
    )(zx, zx, zx, zx, dt, conv_w, cb2, dtb, alog, dexp, ng, tril, e3, sel3)


def _proj_ln_kernel(y_ref, w_ref, x_ref, gate_ref, g_ref, b_ref, o_ref):
    acc = jnp.dot(y_ref[...], w_ref[...], preferred_element_type=F32)
    u = DEEPNORM_ALPHA * x_ref[...] + (1.0 + gate_ref[...]) * acc
    o_ref[...] = _layer_norm(u, g_ref[...], b_ref[...])


def _proj_ln(y, w, x2, gate, ln_g, ln_b, *, seq, tm):
    t, kdim = y.shape
    d = w.shape[1]
    per_batch = seq // tm
    return pl.pallas_call(
        _proj_ln_kernel,
        grid=(t // tm,),
        in_specs=[
            pl.BlockSpec((tm, kdim), lambda i: (i, 0)),
            pl.BlockSpec((kdim, d), lambda i: (0, 0), pipeline_mode=pl.Buffered(1)),
            pl.BlockSpec((tm, d), lambda i: (i, 0)),
            pl.BlockSpec((None, 1, d), lambda i: (i // per_batch, 0, 0)),
            pl.BlockSpec((1, d), lambda i: (0, 0)),
            pl.BlockSpec((1, d), lambda i: (0, 0)),
        ],
        out_specs=pl.BlockSpec((tm, d), lambda i: (i, 0)),
        out_shape=jax.ShapeDtypeStruct((t, d), F32),
        compiler_params=_params(("parallel",)),
        name="out_proj_ln",
    )(y, w, x2, gate, ln_g.reshape(1, d), ln_b.reshape(1, d))


def _b_proj_kernel(x_ref, scale_ref, shift_ref, w_ref, o0_ref, o1_ref, o2_ref, z_ref, h_ref, res_ref, tmp_ref):
    tm = x_ref.shape[0]
    n_qkv = 3 * DIL_N_GROUPS
    ow = DIL_OUT_WIDTH
    block_tiles = ow // LANES
    tiles = 2
    n_chunks = w_ref.shape[1] // (tiles * LANES)
    step_blocks = w_ref.shape[1] // ow
    out_refs = (o0_ref, o1_ref, o2_ref)

    @pl.when(pl.program_id(1) == 0)
    def _():
        x = x_ref[...]
        h_ref[0] = (x * (1.0 + scale_ref[...]) + shift_ref[...]).astype(BF16)
        h_ref[1] = x.astype(BF16)

    def emit_tile(block, gt):
        cols = slice((gt % block_tiles) * LANES, (gt % block_tiles + 1) * LANES)
        if block == n_qkv:
            z_ref[:, cols] = res_ref[gt]
            return
        out_ref = out_refs[block % DIL_N_GROUPS]
        dilation = DIL_PATTERNS[block % DIL_N_GROUPS][1]
        rows = tm // dilation
        if dilation <= SHUFFLE_STRIDE:
            for r in range(dilation):
                out_ref[r, :, cols] = res_ref[gt, pl.ds(r, rows, stride=dilation), :].astype(BF16)
        else:
            assert dilation == SHUFFLE_STRIDE * SHUFFLE_STRIDE
            tmp = tmp_ref.at[gt % 2]
            for r_lo in range(SHUFFLE_STRIDE):
                tmp[r_lo] = res_ref[gt, pl.ds(r_lo, tm // SHUFFLE_STRIDE, stride=SHUFFLE_STRIDE), :]
            for r_lo in range(SHUFFLE_STRIDE):
                for r_hi in range(SHUFFLE_STRIDE):
                    out_ref[r_lo + SHUFFLE_STRIDE * r_hi, :, cols] = (
                        tmp[r_lo, pl.ds(r_hi, rows, stride=SHUFFLE_STRIDE), :].astype(BF16))

    def step(jj):
        def block_of(gt):
            return jj * step_blocks + gt // block_tiles

        def matmul_chunk(c):
            block = block_of(c * tiles)
            plain = 1 if DIL_N_GROUPS <= block < n_qkv else 0
            cols = slice(c * tiles * LANES, (c + 1) * tiles * LANES)
            acc = jnp.dot(h_ref[plain], w_ref[:, cols], preferred_element_type=F32)
            for t in range(tiles):
                res_ref[c * tiles + t] = acc[:, t * LANES:(t + 1) * LANES]

        matmul_chunk(0)
        for c in range(1, n_chunks + 1):
            if c < n_chunks:
                matmul_chunk(c)
            for gt in range((c - 1) * tiles, c * tiles):
                emit_tile(block_of(gt), gt)

    for jj in range((n_qkv + 1) // step_blocks):
        pl.when(pl.program_id(1) == jj)(lambda jj=jj: step(jj))


def _b_proj(x2, scale, shift, w, *, batch, seq, tm):
    t, d = x2.shape
    ow = DIL_OUT_WIDTH
    per_batch = seq // tm
    step_blocks = 2
    assert DIL_N_GROUPS == 3 and w.shape[1] == (3 * DIL_N_GROUPS + 1) * ow
    part_at_step = (lambda j: jnp.minimum((j + 1) // 2, 2),
                    lambda j: jnp.clip(j - 1, 0, 2),
                    lambda j: j // 2)
    out_specs, out_shape = [], []
    for (_, dilation), part in zip(DIL_PATTERNS, part_at_step):
        out_specs.append(pl.BlockSpec(
            (None, dilation, tm // dilation, ow),
            lambda i, j, part=part: (i // per_batch, 0, i % per_batch, part(j))))
        out_shape.append(jax.ShapeDtypeStruct((batch, dilation, seq // dilation, 3 * ow), BF16))
    out_specs.append(pl.BlockSpec((tm, ow), lambda i, j: (i, 0)))
    out_shape.append(jax.ShapeDtypeStruct((t, ow), F32))
    return pl.pallas_call(
        _b_proj_kernel,
        grid=(t // tm, w.shape[1] // (step_blocks * ow)),
        in_specs=[
            pl.BlockSpec((tm, d), lambda i, j: (i, 0)),
            pl.BlockSpec((None, 1, d), lambda i, j: (i // per_batch, 0, 0)),
            pl.BlockSpec((None, 1, d), lambda i, j: (i // per_batch, 0, 0)),
            pl.BlockSpec((d, step_blocks * ow), lambda i, j: (0, j)),
        ],
        out_specs=out_specs,
        out_shape=out_shape,
        scratch_shapes=[pltpu.VMEM((2, tm, d), BF16), pltpu.VMEM((step_blocks * ow // LANES, tm, LANES), F32),
                        pltpu.VMEM((2, SHUFFLE_STRIDE, tm // SHUFFLE_STRIDE, LANES), F32)],
        compiler_params=_params(("parallel", "arbitrary")),
        name="dilated_in_proj",
    )(x2, scale, shift, w)


def _attn_bias(group):
    _, dilation = DIL_PATTERNS[group]
    n_all = DIL_N_GROUPS * DIL_HEADS
    slopes = 2.0 ** (-8.0 * np.arange(1, n_all + 1) / n_all)
    slopes = slopes.reshape(DIL_N_GROUPS, DIL_HEADS)[group].astype(np.float32)
    qi = np.arange(DIL_BLOCK)[:, None]
    kj = np.arange(2 * DIL_BLOCK)[None, :]
    delta = qi + DIL_BLOCK - kj
    valid = (delta >= 0) & (delta <= DIL_BLOCK)
    alibi = -slopes[:, None, None] * (delta * dilation).astype(np.float32)[None]
    return jnp.asarray(np.where(valid[None], alibi * np.float32(LOG2E), -np.inf).astype(np.float32))


def _attn_kernel(q_ref, k_ref, v_ref, kp_ref, vp_ref, bias_ref, o_ref, st_ref):
    first = pl.program_id(2) == 0
    blk = DIL_BLOCK
    n_sub, tq = q_ref.shape[0], q_ref.shape[1]
    scale = DIL_HEAD_DIM ** -0.5 * LOG2E
    dn = (((1,), (1,)), ((), ()))
    lane = lax.broadcasted_iota(jnp.int32, (blk, LANES), 1)
    kcol = lax.broadcasted_iota(jnp.int32, (blk, 2 * blk), 1)
    no_prev = first & (kcol < blk)
    for r in range(n_sub):
        for sb in range(tq // blk):
            rows = slice(sb * blk, (sb + 1) * blk)
            keys = slice((sb - 1) * blk, (sb + 1) * blk)
            m_tile = jnp.zeros((blk, LANES), F32)
            den_tile = jnp.ones((blk, LANES), F32)
            ss = []
            for h in range(DIL_HEADS):
                cols = slice(h * DIL_HEAD_DIM, (h + 1) * DIL_HEAD_DIM)
                q = q_ref[r, rows, cols]
                if sb == 0:
                    s = jnp.concatenate(
                        [lax.dot_general(q, kp_ref[r, :, cols], dn, preferred_element_type=F32),
                         lax.dot_general(q, k_ref[r, rows, cols], dn, preferred_element_type=F32)], axis=1)
                else:
                    s = lax.dot_general(q, k_ref[r, keys, cols], dn, preferred_element_type=F32)
                ss.append(s)
            ps = []
            for h in range(DIL_HEADS):
                bias = bias_ref[h]
                if sb == 0:
                    bias = jnp.where(no_prev, -jnp.inf, bias)
                s = ss[h] * scale + bias
                m = jnp.max(s, axis=-1, keepdims=True)
                p = jnp.exp2(s - m)
                den = jnp.sum(p, axis=-1, keepdims=True)
                ps.append(p.astype(BF16))
                m_tile = jnp.where(lane == h, m, m_tile)
                den_tile = jnp.where(lane == h, den, den_tile)
            for h in range(DIL_HEADS):
                cols = slice(h * DIL_HEAD_DIM, (h + 1) * DIL_HEAD_DIM)
                p16 = ps[h]
                if sb == 0:
                    o = (jnp.dot(p16[:, :blk], vp_ref[r, :, cols], preferred_element_type=F32)
                         + jnp.dot(p16[:, blk:], v_ref[r, rows, cols], preferred_element_type=F32))
                else:
                    o = jnp.dot(p16, v_ref[r, keys, cols], preferred_element_type=F32)
                o_ref[r, rows, cols] = o
            st_ref[r, rows, :LANES] = (m_tile + jnp.log2(den_tile)) * (1.0 / LOG2E)
            st_ref[r, rows, LANES:] = 1.0 / den_tile


def _dilated_attention(qkv, group, *, n_sub, tq):
    batch, dilation, m, _ = qkv.shape
    assert DIL_PATTERNS[group][0] // dilation == DIL_BLOCK and m % tq == 0 and dilation % n_sub == 0
    w = DIL_OUT_WIDTH
    blk = DIL_BLOCK
    sub = tq // blk

    def prev(i):
        return jnp.maximum(i * sub - 1, 0)

    return pl.pallas_call(
        _attn_kernel,
        grid=(batch, dilation // n_sub, m // tq),
        in_specs=[
            pl.BlockSpec((None, n_sub, tq, w), lambda b, r, i: (b, r, i, 0)),
            pl.BlockSpec((None, n_sub, tq, w), lambda b, r, i: (b, r, i, 1)),
            pl.BlockSpec((None, n_sub, tq, w), lambda b, r, i: (b, r, i, 2)),
            pl.BlockSpec((None, n_sub, blk, w), lambda b, r, i: (b, r, prev(i), 1)),
            pl.BlockSpec((None, n_sub, blk, w), lambda b, r, i: (b, r, prev(i), 2)),
            pl.BlockSpec((DIL_HEADS, blk, 2 * blk), lambda b, r, i: (0, 0, 0)),
        ],
        out_specs=[
            pl.BlockSpec((None, n_sub, tq, w), lambda b, r, i: (b, r, i, 0)),
            pl.BlockSpec((None, n_sub, tq, 2 * LANES), lambda b, r, i: (b, r, i, 0)),
        ],
        out_shape=[
            jax.ShapeDtypeStruct((batch, dilation, m, w), F32),
            jax.ShapeDtypeStruct((batch, dilation, m, 2 * LANES), F32),
        ],
        compiler_params=_params(("parallel", "parallel", "arbitrary")),
        name=f"dilated_attn_{group}",
    )(qkv, qkv, qkv, qkv, qkv, _attn_bias(group))


def _merge_ln_kernel(o0_ref, o1_ref, o2_ref, s0_ref, s1_ref, s2_ref, z_ref, w_ref, x_ref,
                     gate_ref, g_ref, b_ref, out_ref, on_ref, sn_ref, tmp_ref):
    tm = x_ref.shape[0]
    assert DIL_PATTERNS[0][1] == 1
    def scatter(dst, src_ref, cols, dilation, tmp):
        rows = tm // dilation
        if dilation <= SHUFFLE_STRIDE:
            for r in range(dilation):
                dst[pl.ds(r, rows, stride=dilation), :] = src_ref[r, :, cols]
        else:
            assert dilation == SHUFFLE_STRIDE * SHUFFLE_STRIDE
            for r_lo in range(SHUFFLE_STRIDE):
                for r_hi in range(SHUFFLE_STRIDE):
                    tmp[r_lo, pl.ds(r_hi, rows, stride=SHUFFLE_STRIDE), :] = (
                        src_ref[r_lo + SHUFFLE_STRIDE * r_hi, :, cols])
            for r_lo in range(SHUFFLE_STRIDE):
                dst[pl.ds(r_lo, tm // SHUFFLE_STRIDE, stride=SHUFFLE_STRIDE), :] = tmp[r_lo]

    n_tmp = tmp_ref.shape[0]
    for g, (o_ref, s_ref) in ((1, (o1_ref, s1_ref)), (2, (o2_ref, s2_ref))):
        dilation = DIL_PATTERNS[g][1]
        for half in range(2):
            scatter(sn_ref.at[g - 1, half], s_ref, slice(half * LANES, (half + 1) * LANES), dilation,
                    tmp_ref.at[half % n_tmp])
        for h in range(DIL_HEADS):
            scatter(on_ref.at[g - 1, h], o_ref, slice(h * DIL_HEAD_DIM, (h + 1) * DIL_HEAD_DIM), dilation,
                    tmp_ref.at[h % n_tmp])
    l0, l1, l2 = s0_ref[0, :, :LANES], sn_ref[0, 0], sn_ref[1, 0]
    mx = jnp.maximum(jnp.maximum(l0, l1), l2)
    e0, e1, e2 = jnp.exp(l0 - mx), jnp.exp(l1 - mx), jnp.exp(l2 - mx)
    inv = 1.0 / (e0 + e1 + e2)
    c0 = e0 * inv * s0_ref[0, :, LANES:]
    c1 = e1 * inv * sn_ref[0, 1]
    c2 = e2 * inv * sn_ref[1, 1]
    pieces = []
    for h in range(DIL_HEADS):
        pieces.append(o0_ref[0, :, h * DIL_HEAD_DIM:(h + 1) * DIL_HEAD_DIM] * c0[:, h:h + 1]
                      + on_ref[0, h] * c1[:, h:h + 1]
                      + on_ref[1, h] * c2[:, h:h + 1])
    hz = 0.5 * z_ref[...]
    o = jnp.concatenate(pieces, axis=1) * (hz + hz * jnp.tanh(hz))
    y = jnp.dot(o.astype(BF16), w_ref[...], preferred_element_type=F32)
    u = DEEPNORM_ALPHA * x_ref[...] + (1.0 + gate_ref[...]) * y
    out_ref[...] = _layer_norm(u, g_ref[...], b_ref[...])


def _merge_ln(os_, stats, z, w, x2, gate, ln_g, ln_b, *, seq, tm):
    t, d = x2.shape
    ow = DIL_OUT_WIDTH
    per_batch = seq // tm
    row = lambda i: (i, 0)

    def sub_major(width):
        return [pl.BlockSpec((None, dilation, tm // dilation, width),
                             lambda i: (i // per_batch, 0, i % per_batch, 0))
                for _, dilation in DIL_PATTERNS]

    return pl.pallas_call(
        _merge_ln_kernel,
        grid=(t // tm,),
        in_specs=sub_major(ow) + sub_major(2 * LANES) + [
            pl.BlockSpec((tm, ow), row),
            pl.BlockSpec((ow, d), lambda i: (0, 0)),
            pl.BlockSpec((tm, d), row),
            pl.BlockSpec((None, 1, d), lambda i: (i // per_batch, 0, 0)),
            pl.BlockSpec((1, d), lambda i: (0, 0)),
            pl.BlockSpec((1, d), lambda i: (0, 0)),
        ],
        out_specs=pl.BlockSpec((tm, d), row),
        out_shape=jax.ShapeDtypeStruct((t, d), F32),
        scratch_shapes=[pltpu.VMEM((DIL_N_GROUPS - 1, DIL_HEADS, tm, LANES), F32),
                        pltpu.VMEM((DIL_N_GROUPS - 1, 2, tm, LANES), F32),
                        pltpu.VMEM((2, SHUFFLE_STRIDE, tm // SHUFFLE_STRIDE, LANES), F32)],
        compiler_params=_params(("parallel",)),
        name="merge_out_proj_ln",
    )(*os_, *stats, z, w, x2, gate, ln_g.reshape(1, d), ln_b.reshape(1, d))


def _ssd_block(x2, scale, shift, gate, in_w, conv_w, conv_b, dt_bias, a_log, d_skip, norm_g, out_w, ln_g, ln_b,
               *, batch, seq):
    w16 = in_w.astype(BF16)
    dt_w = jnp.pad(w16[:, SSD_D_INNER + SSD_CONV_DIM:], ((0, 0), (0, LANES - SSD_N_HEADS)))
    zx, dt = _in_proj(x2, scale, shift, w16, dt_w, n=SSD_D_INNER + SSD_CONV_DIM, seq=seq, tm=1024, tn=1280)
    y = _ssd_scan(zx, dt, conv_w, conv_b, dt_bias, a_log, d_skip, norm_g, batch=batch, seq=seq)
    return _proj_ln(y, out_w.astype(BF16), x2, gate, ln_g, ln_b, seq=seq, tm=512)


def _dilated_block(x2, scale, shift, gate, kv_w, in_w, out_w, ln_g, ln_b, *, batch, seq):
    w = jnp.concatenate([in_w[:, :DIL_Q_WIDTH], kv_w, in_w[:, DIL_Q_WIDTH:]], axis=1).astype(BF16)
    *qkvs, z = _b_proj(x2, scale, shift, w, batch=batch, seq=seq, tm=512)
    os_, stats = [], []
    for group in range(DIL_N_GROUPS):
        tq = min(ATTN_ROWS, seq // DIL_PATTERNS[group][1])
        o, st = _dilated_attention(qkvs[group], group, n_sub=ATTN_ROWS // tq, tq=tq)
        os_.append(o)
        stats.append(st)
    return _merge_ln(os_, stats, z, out_w.astype(BF16), x2, gate, ln_g, ln_b, seq=seq, tm=512)


def kernel(x, c, ada_w, ada_b, ln_g, ln_b, a_in_w, a_conv_w, a_conv_b, a_dt_bias, a_A_log, a_D,
           a_norm_g, a_out_w, kv_w, b_in_w, b_out_w):
    batch, seq, d = x.shape
    x2 = x.reshape(batch * seq, d)

    c_pad = jnp.pad(c, ((0, 2 * SUBLANES - batch), (0, 0)))
    mod = _adaln(c_pad, ada_w, ada_b)[:, :batch]
    shift = mod[:, :, None, 0:d]
    scale = mod[:, :, None, d:2 * d]
    gate = mod[:, :, None, 2 * d:3 * d]

    x2 = _ssd_block(x2, scale[0], shift[0], gate[0], a_in_w[0], a_conv_w[0], a_conv_b[0], a_dt_bias[0],
                    a_A_log[0], a_D[0], a_norm_g[0], a_out_w[0], ln_g[0], ln_b[0], batch=batch, seq=seq)
    x2 = _dilated_block(x2, scale[1], shift[1], gate[1], kv_w, b_in_w[0], b_out_w[0], ln_g[1], ln_b[1],
                        batch=batch, seq=seq)
    return x2.reshape(batch, seq, d)
```

```python
import numpy as np
import jax
import jax.numpy as jnp
from jax import lax
from jax.experimental import pallas as pl
from jax.experimental.pallas import tpu as pltpu

F32 = jnp.float32
BF16 = jnp.bfloat16

D_MODEL = 2048
DEPTH = 2
LANES = 128
SUBLANES = 8

SSD_D_INNER = 2 * D_MODEL
SSD_HEAD_DIM = 64
SSD_N_HEADS = SSD_D_INNER // SSD_HEAD_DIM
SSD_N_GROUPS = 8
SSD_HEADS_PER_GROUP = SSD_N_HEADS // SSD_N_GROUPS
SSD_D_STATE = 128
SSD_CONV_W = 4
SSD_CHUNK = 256
SSD_BC_DIM = SSD_N_GROUPS * SSD_D_STATE
SSD_CONV_DIM = SSD_D_INNER + 2 * SSD_BC_DIM
SSD_GROUP_W = SSD_HEADS_PER_GROUP * SSD_HEAD_DIM

DIL_PATTERNS = ((128, 1), (512, 4), (2048, 16))
DIL_N_GROUPS = len(DIL_PATTERNS)
DIL_HEADS = 8
DIL_HEAD_DIM = 128
DIL_Q_WIDTH = DIL_N_GROUPS * DIL_HEADS * DIL_HEAD_DIM
DIL_OUT_WIDTH = DIL_HEADS * DIL_HEAD_DIM
DIL_BLOCK = 128
SHUFFLE_STRIDE = 4
ATTN_ROWS = 1024

DEEPNORM_ALPHA = (2 * DEPTH) ** 0.25
LN_EPS = 1e-5
RMS_EPS = 1e-5
LOG2E = 1.4426950408889634

VMEM_LIMIT = 56 * 1024 * 1024


def _params(sem):
    return pltpu.CompilerParams(dimension_semantics=sem, vmem_limit_bytes=VMEM_LIMIT)


def _silu(v):
    return v * (1.0 / (1.0 + jnp.exp(-v)))


def _split3(v):
    v1 = v.astype(BF16)
    r1 = v - v1.astype(F32)
    v2 = r1.astype(BF16)
    r2 = r1 - v2.astype(F32)
    v3 = r2.astype(BF16)
    return jnp.concatenate([v1, v2, v3], axis=1)


def _layer_norm(u, g, b):
    mu = jnp.mean(u, axis=-1, keepdims=True)
    d = u - mu
    var = jnp.mean(d * d, axis=-1, keepdims=True)
    return d * lax.rsqrt(var + LN_EPS) * g + b


def _adaln_kernel(c_ref, w_ref, b_ref, o_ref):
    s = _silu(c_ref[...]).astype(BF16)
    o_ref[...] = jnp.dot(s, w_ref[...].astype(BF16), preferred_element_type=F32) + b_ref[...]


def _adaln(c_pad, ada_w, ada_b):
    rows = c_pad.shape[0]
    n = 3 * D_MODEL
    tn = 768
    return pl.pallas_call(
        _adaln_kernel,
        grid=(DEPTH, n // tn),
        in_specs=[
            pl.BlockSpec((rows, D_MODEL), lambda l, j: (0, 0)),
            pl.BlockSpec((None, D_MODEL, tn), lambda l, j: (l, 0, j)),
            pl.BlockSpec((None, 1, tn), lambda l, j: (l, 0, j)),
        ],
        out_specs=pl.BlockSpec((None, rows, tn), lambda l, j: (l, 0, j)),
        out_shape=jax.ShapeDtypeStruct((DEPTH, rows, n), F32),
        compiler_params=_params(("parallel", "parallel")),
        name="adaln",
    )(c_pad, ada_w, ada_b.reshape(DEPTH, 1, n))


def _in_proj_kernel(x_ref, scale_ref, shift_ref, w_ref, wdt_ref, o_ref, dt_ref, h_ref):
    @pl.when(pl.program_id(1) == 0)
    def _():
        h = (x_ref[...] * (1.0 + scale_ref[...]) + shift_ref[...]).astype(BF16)
        h_ref[...] = h
        dt_ref[...] = jnp.dot(h, wdt_ref[...], preferred_element_type=F32)

    o_ref[...] = jnp.dot(h_ref[...], w_ref[...], preferred_element_type=F32)


def _in_proj(x2, scale, shift, w, wdt, *, n, seq, tm, tn):
    t, d = x2.shape
    per_batch = seq // tm
    return pl.pallas_call(
        _in_proj_kernel,
        grid=(t // tm, n // tn),
        in_specs=[
            pl.BlockSpec((tm, d), lambda i, j: (i, 0)),
            pl.BlockSpec((None, 1, d), lambda i, j: (i // per_batch, 0, 0)),
            pl.BlockSpec((None, 1, d), lambda i, j: (i // per_batch, 0, 0)),
            pl.BlockSpec((d, tn), lambda i, j: (0, j)),
            pl.BlockSpec((d, LANES), lambda i, j: (0, 0)),
        ],
        out_specs=[pl.BlockSpec((tm, tn), lambda i, j: (i, j)), pl.BlockSpec((tm, LANES), lambda i, j: (i, 0))],
        out_shape=[jax.ShapeDtypeStruct((t, n), F32), jax.ShapeDtypeStruct((t, LANES), F32)],
        scratch_shapes=[pltpu.VMEM((tm, d), BF16)],
        compiler_params=_params(("parallel", "arbitrary")),
        name="ssd_in_proj",
    )(x2, scale, shift, w, wdt)


def _conv_silu(raw, carry, w, b):
    assert SSD_CONV_W == 4
    wh = 0.5 * w
    row = lax.broadcasted_iota(jnp.int32, carry.shape, 0)

    def delayed(cur, tail, s):
        sh = pltpu.roll(cur, s, axis=0)
        prev = pltpu.roll(tail, s, axis=0)
        head = jnp.where(row < s, prev, sh[:SUBLANES])
        return jnp.concatenate([head, sh[SUBLANES:]], axis=0)

    x1 = delayed(raw, carry, 1)
    near = raw * wh[3:4, :] + x1 * wh[2:3, :] + 0.5 * b
    far = raw * wh[1:2, :] + x1 * wh[0:1, :]
    far_tail = carry * wh[1:2, :] + pltpu.roll(carry, 1, axis=0) * wh[0:1, :]
    acc = near + delayed(far, far_tail, 2)
    return acc + acc * jnp.tanh(acc)


def _ssd_kernel(z_ref, x_ref, b_ref, c_ref, dt_ref, cw_ref, cb_ref, dtb_ref, alog_ref, dexp_ref, ng_ref,
                tril_ref, e_ref, sel_ref,
                y_ref,
                state_ref, carx_ref, carb_ref, carc_ref, v3_ref, ac3_ref, ybuf_ref):
    L = SSD_CHUNK
    gw = SSD_GROUP_W
    ns = SSD_D_STATE

    @pl.when(pl.program_id(1) == 0)
    def _():
        state_ref[...] = jnp.zeros(state_ref.shape, F32)
        carx_ref[...] = jnp.zeros(carx_ref.shape, F32)
        carb_ref[...] = jnp.zeros(carb_ref.shape, F32)
        carc_ref[...] = jnp.zeros(carc_ref.shape, F32)

    v = dt_ref[...] + dtb_ref[...]
    dt = jnp.maximum(v, 0.0) + jnp.log1p(jnp.exp(-jnp.abs(v)))
    a = dt * (-jnp.exp(alog_ref[...]))
    a3 = _split3(a)
    tril = tril_ref[...]
    acum = (jnp.dot(tril, a3[:, :LANES], preferred_element_type=F32)
            + jnp.dot(tril, a3[:, LANES:2 * LANES], preferred_element_type=F32)
            + jnp.dot(tril, a3[:, 2 * LANES:], preferred_element_type=F32))
    tail = jnp.exp(acum[L - 1:L, :] - acum)
    v3_ref[0:L, :] = _split3(dt)
    v3_ref[L:2 * L, :] = _split3(tail)
    v3_ref[2 * L:3 * L, :] = _split3(jnp.exp(acum))
    ac3_ref[...] = _split3(acum)

    li = lax.broadcasted_iota(jnp.int32, (L, L), 0)
    si = lax.broadcasted_iota(jnp.int32, (L, L), 1)
    causal = li >= si
    lane = lax.broadcasted_iota(jnp.int32, (L, LANES), 1)
    heads_per_tile = LANES // SSD_HEAD_DIM
    assert heads_per_tile == 2
    n_pairs = gw // LANES

    for g in range(SSD_N_GROUPS):
        xs = slice(g * gw, (g + 1) * gw)
        bs = slice(g * ns, (g + 1) * ns)
        wb = slice(SSD_D_INNER + g * ns, SSD_D_INNER + (g + 1) * ns)
        wc = slice(SSD_D_INNER + SSD_BC_DIM + g * ns, SSD_D_INNER + SSD_BC_DIM + (g + 1) * ns)

        ex = jnp.dot(v3_ref[...], e_ref[:, xs], preferred_element_type=F32)
        dt_e = ex[0:L]
        tail_e = ex[L:2 * L]
        eac_e = ex[2 * L:3 * L]
        acg = jnp.dot(ac3_ref[...], sel_ref[:, bs], preferred_element_type=F32) * LOG2E
        acg_t = acg.T

        b_raw = b_ref[:, bs]
        c_raw = c_ref[:, bs]
        bg = _conv_silu(b_raw, carb_ref[:, bs], cw_ref[:, wb], cb_ref[:, wb])
        cg = _conv_silu(c_raw, carc_ref[:, bs], cw_ref[:, wc], cb_ref[:, wc])
        carb_ref[:, bs] = b_raw[L - SUBLANES:, :]
        carc_ref[:, bs] = c_raw[L - SUBLANES:, :]
        cb16 = cg.astype(BF16)
        bb16 = bg.astype(BF16)
        cb = lax.dot_general(cb16, bb16, (((1,), (1,)), ((), ())), preferred_element_type=F32)
        cb = jnp.where(causal, cb, 0.0)

        state = state_ref[g]
        y_state = jnp.dot(cb16, state.astype(BF16), preferred_element_type=F32)

        ssq = jnp.zeros((L, LANES), F32)
        for pair in range(n_pairs):
            ps = slice(pair * LANES, (pair + 1) * LANES)
            cs = slice(g * gw + pair * LANES, g * gw + (pair + 1) * LANES)
            x_raw = x_ref[:, cs]
            xg = _conv_silu(x_raw, carx_ref[:, cs], cw_ref[:, cs], cb_ref[:, cs])
            carx_ref[:, cs] = x_raw[L - SUBLANES:, :]
            xdt = xg * dt_e[:, ps]
            ms = []
            for sub in range(heads_per_tile):
                k = pair * heads_per_tile + sub
                seg = acg[:, k:k + 1] - acg_t[k:k + 1, :]
                ms.append((cb * jnp.exp2(jnp.minimum(seg, 0.0))).astype(BF16))
            both = jnp.dot(jnp.concatenate(ms, axis=0), xdt.astype(BF16), preferred_element_type=F32)
            y = (jnp.where(lane < SSD_HEAD_DIM, both[:L], both[L:]) + y_state[:, ps] * eac_e[:, ps]
                 + xg * dexp_ref[:, cs])
            xtail = (xdt * tail_e[:, ps]).astype(BF16)
            upd = lax.dot_general(bb16, xtail, (((0,), (0,)), ((), ())), preferred_element_type=F32)
            state_ref[g, :, ps] = state[:, ps] * eac_e[L - 1:L, ps] + upd
            hz = 0.5 * z_ref[:, cs]
            y = y * (hz + hz * jnp.tanh(hz))
            ssq = ssq + y * y
            ybuf_ref[:, ps] = y
        r = lax.rsqrt(jnp.sum(ssq, axis=-1, keepdims=True) * (1.0 / gw) + RMS_EPS)
        y_ref[:, xs] = (ybuf_ref[...] * r * ng_ref[:, xs]).astype(y_ref.dtype)


def _ssd_constants():
    tril = np.tril(np.ones((SSD_CHUNK, SSD_CHUNK), np.float32))
    head_of_col = np.arange(SSD_D_INNER) // SSD_HEAD_DIM
    e = (np.arange(LANES)[:, None] == head_of_col[None, :]).astype(np.float32)
    col = np.arange(SSD_N_GROUPS * LANES)
    src = np.where(col % LANES < SSD_HEADS_PER_GROUP,
                   (col // LANES) * SSD_HEADS_PER_GROUP + col % LANES, -1)
    sel = (np.arange(LANES)[:, None] == src[None, :]).astype(np.float32)
    e3 = np.concatenate([e, e, e], axis=0)
    sel3 = np.concatenate([sel, sel, sel], axis=0)
    return jnp.asarray(tril, BF16), jnp.asarray(e3, BF16), jnp.asarray(sel3, BF16)


def _ssd_scan(zx, dt, conv_w, conv_b, dt_bias, a_log, d_skip, norm_g, *, batch, seq):
    t = zx.shape[0]
    L = SSD_CHUNK
    nc = seq // L
    G = SSD_N_GROUPS
    gw = SSD_GROUP_W
    ns = SSD_D_STATE
    tril, e3, sel3 = _ssd_constants()
    pad = LANES - SSD_N_HEADS
    dtb = jnp.pad(dt_bias, (0, pad)).reshape(1, LANES)
    alog = jnp.pad(a_log, (0, pad)).reshape(1, LANES)
    dexp = jnp.repeat(d_skip, SSD_HEAD_DIM).reshape(1, SSD_D_INNER)
    ng = norm_g.reshape(1, SSD_D_INNER)
    cb2 = conv_b.reshape(1, SSD_CONV_DIM)

    zw = SSD_D_INNER
    const = lambda b, c: (0, 0)
    in_specs = [
        pl.BlockSpec((L, zw), lambda b, c: (b * nc + c, 0)),
        pl.BlockSpec((L, zw), lambda b, c: (b * nc + c, 1)),
        pl.BlockSpec((L, SSD_BC_DIM), lambda b, c: (b * nc + c, 2 * zw // SSD_BC_DIM)),
        pl.BlockSpec((L, SSD_BC_DIM), lambda b, c: (b * nc + c, 2 * zw // SSD_BC_DIM + 1)),
        pl.BlockSpec((L, LANES), lambda b, c: (b * nc + c, 0)),
        pl.BlockSpec((SSD_CONV_W, SSD_CONV_DIM), const),
        pl.BlockSpec((1, SSD_CONV_DIM), const),
        pl.BlockSpec((1, LANES), const),
        pl.BlockSpec((1, LANES), const),
        pl.BlockSpec((1, zw), const),
        pl.BlockSpec((1, zw), const),
        pl.BlockSpec((L, L), const),
        pl.BlockSpec((3 * LANES, zw), const),
        pl.BlockSpec((3 * LANES, G * LANES), const),
    ]
    return pl.pallas_call(
        _ssd_kernel,
        grid=(batch, nc),
        in_specs=in_specs,
        out_specs=pl.BlockSpec((L, zw), lambda b, c: (b * nc + c, 0)),
        out_shape=jax.ShapeDtypeStruct((t, zw), BF16),
        scratch_shapes=[
            pltpu.VMEM((G, ns, gw), F32),
            pltpu.VMEM((SUBLANES, zw), F32),
            pltpu.VMEM((SUBLANES, SSD_BC_DIM), F32),
            pltpu.VMEM((SUBLANES, SSD_BC_DIM), F32),
            pltpu.VMEM((3 * L, 3 * LANES), BF16),
            pltpu.VMEM((L, 3 * LANES), BF16),
            pltpu.VMEM((L, gw), F32),
        ],
        compiler_params=_params(("parallel", "arbitrary")),
        name="ssd_scan",
    )(zx, zx, zx, zx, dt, conv_w, cb2, dtb, alog, dexp, ng, tril, e3, sel3)


def _proj_ln_kernel(y_ref, w_ref, x_ref, gate_ref, g_ref, b_ref, o_ref):
    acc = jnp.dot(y_ref[...], w_ref[...], preferred_element_type=F32)
    u = DEEPNORM_ALPHA * x_ref[...] + (1.0 + gate_ref[...]) * acc
    o_ref[...] = _layer_norm(u, g_ref[...], b_ref[...])


def _proj_ln(y, w, x2, gate, ln_g, ln_b, *, seq, tm):
    t, kdim = y.shape
    d = w.shape[1]
    per_batch = seq // tm
    return pl.pallas_call(
        _proj_ln_kernel,
        grid=(t // tm,),
        in_specs=[
            pl.BlockSpec((tm, kdim), lambda i: (i, 0)),
            pl.BlockSpec((kdim, d), lambda i: (0, 0), pipeline_mode=pl.Buffered(1)),
            pl.BlockSpec((tm, d), lambda i: (i, 0)),
            pl.BlockSpec((None, 1, d), lambda i: (i // per_batch, 0, 0)),
            pl.BlockSpec((1, d), lambda i: (0, 0)),
            pl.BlockSpec((1, d), lambda i: (0, 0)),
        ],
        out_specs=pl.BlockSpec((tm, d), lambda i: (i, 0)),
        out_shape=jax.ShapeDtypeStruct((t, d), F32),
        compiler_params=_params(("parallel",)),
        name="out_proj_ln",
    )(y, w, x2, gate, ln_g.reshape(1, d), ln_b.reshape(1, d))


def _b_proj_kernel(x_ref, scale_ref, shift_ref, w_ref, o0_ref, o1_ref, o2_ref, z_ref, h_ref, res_ref, tmp_ref):
    tm = x_ref.shape[0]
    n_qkv = 3 * DIL_N_GROUPS
    ow = DIL_OUT_WIDTH
    block_tiles = ow // LANES
    tiles = 2
    n_chunks = w_ref.shape[1] // (tiles * LANES)
    step_blocks = w_ref.shape[1] // ow
    out_refs = (o0_ref, o1_ref, o2_ref)

    @pl.when(pl.program_id(1) == 0)
    def _():
        x = x_ref[...]
        h_ref[0] = (x * (1.0 + scale_ref[...]) + shift_ref[...]).astype(BF16)
        h_ref[1] = x.astype(BF16)

    def emit_tile(block, gt):
        cols = slice((gt % block_tiles) * LANES, (gt % block_tiles + 1) * LANES)
        if block == n_qkv:
            z_ref[:, cols] = res_ref[gt]
            return
        out_ref = out_refs[block % DIL_N_GROUPS]
        dilation = DIL_PATTERNS[block % DIL_N_GROUPS][1]
        rows = tm // dilation
        if dilation <= SHUFFLE_STRIDE:
            for r in range(dilation):
                out_ref[r, :, cols] = res_ref[gt, pl.ds(r, rows, stride=dilation), :].astype(BF16)
        else:
            assert dilation == SHUFFLE_STRIDE * SHUFFLE_STRIDE
            tmp = tmp_ref.at[gt % 2]
            for r_lo in range(SHUFFLE_STRIDE):
                tmp[r_lo] = res_ref[gt, pl.ds(r_lo, tm // SHUFFLE_STRIDE, stride=SHUFFLE_STRIDE), :]
            for r_lo in range(SHUFFLE_STRIDE):
                for r_hi in range(SHUFFLE_STRIDE):
                    out_ref[r_lo + SHUFFLE_STRIDE * r_hi, :, cols] = (
                        tmp[r_lo, pl.ds(r_hi, rows, stride=SHUFFLE_STRIDE), :].astype(BF16))

    def step(jj):
        def block_of(gt):
            return jj * step_blocks + gt // block_tiles

        def matmul_chunk(c):
            block = block_of(c * tiles)
            plain = 1 if DIL_N_GROUPS <= block < n_qkv else 0
            cols = slice(c * tiles * LANES, (c + 1) * tiles * LANES)
            acc = jnp.dot(h_ref[plain], w_ref[:, cols], preferred_element_type=F32)
            for t in range(tiles):
                res_ref[c * tiles + t] = acc[:, t * LANES:(t + 1) * LANES]

        matmul_chunk(0)
        for c in range(1, n_chunks + 1):
            if c < n_chunks:
                matmul_chunk(c)
            for gt in range((c - 1) * tiles, c * tiles):
                emit_tile(block_of(gt), gt)

    for jj in range((n_qkv + 1) // step_blocks):
        pl.when(pl.program_id(1) == jj)(lambda jj=jj: step(jj))


def _b_proj(x2, scale, shift, w, *, batch, seq, tm):
    t, d = x2.shape
    ow = DIL_OUT_WIDTH
    per_batch = seq // tm
    step_blocks = 2
    assert DIL_N_GROUPS == 3 and w.shape[1] == (3 * DIL_N_GROUPS + 1) * ow
    part_at_step = (lambda j: jnp.minimum((j + 1) // 2, 2),
                    lambda j: jnp.clip(j - 1, 0, 2),
                    lambda j: j // 2)
    out_specs, out_shape = [], []
    for (_, dilation), part in zip(DIL_PATTERNS, part_at_step):
        out_specs.append(pl.BlockSpec(
            (None, dilation, tm // dilation, ow),
            lambda i, j, part=part: (i // per_batch, 0, i % per_batch, part(j))))
        out_shape.append(jax.ShapeDtypeStruct((batch, dilation, seq // dilation, 3 * ow), BF16))
    out_specs.append(pl.BlockSpec((tm, ow), lambda i, j: (i, 0)))
    out_shape.append(jax.ShapeDtypeStruct((t, ow), F32))
    return pl.pallas_call(
        _b_proj_kernel,
        grid=(t // tm, w.shape[1] // (step_blocks * ow)),
        in_specs=[
            pl.BlockSpec((tm, d), lambda i, j: (i, 0)),
            pl.BlockSpec((None, 1, d), lambda i, j: (i // per_batch, 0, 0)),
            pl.BlockSpec((None, 1, d), lambda i, j: (i // per_batch, 0, 0)),
            pl.BlockSpec((d, step_blocks * ow), lambda i, j: (0, j)),
        ],
        out_specs=out_specs,
        out_shape=out_shape,
        scratch_shapes=[pltpu.VMEM((2, tm, d), BF16), pltpu.VMEM((step_blocks * ow // LANES, tm, LANES), F32),
                        pltpu.VMEM((2, SHUFFLE_STRIDE, tm // SHUFFLE_STRIDE, LANES), F32)],
        compiler_params=_params(("parallel", "arbitrary")),
        name="dilated_in_proj",
    )(x2, scale, shift, w)


def _attn_bias(group):
    _, dilation = DIL_PATTERNS[group]
    n_all = DIL_N_GROUPS * DIL_HEADS
    slopes = 2.0 ** (-8.0 * np.arange(1, n_all + 1) / n_all)
    slopes = slopes.reshape(DIL_N_GROUPS, DIL_HEADS)[group].astype(np.float32)
    qi = np.arange(DIL_BLOCK)[:, None]
    kj = np.arange(2 * DIL_BLOCK)[None, :]
    delta = qi + DIL_BLOCK - kj
    valid = (delta >= 0) & (delta <= DIL_BLOCK)
    alibi = -slopes[:, None, None] * (delta * dilation).astype(np.float32)[None]
    return jnp.asarray(np.where(valid[None], alibi * np.float32(LOG2E), -np.inf).astype(np.float32))


def _attn_kernel(q_ref, k_ref, v_ref, kp_ref, vp_ref, bias_ref, o_ref, st_ref):
    first = pl.program_id(2) == 0
    blk = DIL_BLOCK
    n_sub, tq = q_ref.shape[0], q_ref.shape[1]
    scale = DIL_HEAD_DIM ** -0.5 * LOG2E
    dn = (((1,), (1,)), ((), ()))
    lane = lax.broadcasted_iota(jnp.int32, (blk, LANES), 1)
    kcol = lax.broadcasted_iota(jnp.int32, (blk, 2 * blk), 1)
    no_prev = first & (kcol < blk)
    for r in range(n_sub):
        for sb in range(tq // blk):
            rows = slice(sb * blk, (sb + 1) * blk)
            keys = slice((sb - 1) * blk, (sb + 1) * blk)
            m_tile = jnp.zeros((blk, LANES), F32)
            den_tile = jnp.ones((blk, LANES), F32)
            ss = []
            for h in range(DIL_HEADS):
                cols = slice(h * DIL_HEAD_DIM, (h + 1) * DIL_HEAD_DIM)
                q = q_ref[r, rows, cols]
                if sb == 0:
                    s = jnp.concatenate(
                        [lax.dot_general(q, kp_ref[r, :, cols], dn, preferred_element_type=F32),
                         lax.dot_general(q, k_ref[r, rows, cols], dn, preferred_element_type=F32)], axis=1)
                else:
                    s = lax.dot_general(q, k_ref[r, keys, cols], dn, preferred_element_type=F32)
                ss.append(s)
            ps = []
            for h in range(DIL_HEADS):
                bias = bias_ref[h]
                if sb == 0:
                    bias = jnp.where(no_prev, -jnp.inf, bias)
                s = ss[h] * scale + bias
                m = jnp.max(s, axis=-1, keepdims=True)
                p = jnp.exp2(s - m)
                den = jnp.sum(p, axis=-1, keepdims=True)
                ps.append(p.astype(BF16))
                m_tile = jnp.where(lane == h, m, m_tile)
                den_tile = jnp.where(lane == h, den, den_tile)
            for h in range(DIL_HEADS):
                cols = slice(h * DIL_HEAD_DIM, (h + 1) * DIL_HEAD_DIM)
                p16 = ps[h]
                if sb == 0:
                    o = (jnp.dot(p16[:, :blk], vp_ref[r, :, cols], preferred_element_type=F32)
                         + jnp.dot(p16[:, blk:], v_ref[r, rows, cols], preferred_element_type=F32))
                else:
                    o = jnp.dot(p16, v_ref[r, keys, cols], preferred_element_type=F32)
                o_ref[r, rows, cols] = o
            st_ref[r, rows, :LANES] = (m_tile + jnp.log2(den_tile)) * (1.0 / LOG2E)
            st_ref[r, rows, LANES:] = 1.0 / den_tile


def _dilated_attention(qkv, group, *, n_sub, tq):
    batch, dilation, m, _ = qkv.shape
    assert DIL_PATTERNS[group][0] // dilation == DIL_BLOCK and m % tq == 0 and dilation % n_sub == 0
    w = DIL_OUT_WIDTH
    blk = DIL_BLOCK
    sub = tq // blk

    def prev(i):
        return jnp.maximum(i * sub - 1, 0)

    return pl.pallas_call(
        _attn_kernel,
        grid=(batch, dilation // n_sub, m // tq),
        in_specs=[
            pl.BlockSpec((None, n_sub, tq, w), lambda b, r, i: (b, r, i, 0)),
            pl.BlockSpec((None, n_sub, tq, w), lambda b, r, i: (b, r, i, 1)),
            pl.BlockSpec((None, n_sub, tq, w), lambda b, r, i: (b, r, i, 2)),
            pl.BlockSpec((None, n_sub, blk, w), lambda b, r, i: (b, r, prev(i), 1)),
            pl.BlockSpec((None, n_sub, blk, w), lambda b, r, i: (b, r, prev(i), 2)),
            pl.BlockSpec((DIL_HEADS, blk, 2 * blk), lambda b, r, i: (0, 0, 0)),
        ],
        out_specs=[
            pl.BlockSpec((None, n_sub, tq, w), lambda b, r, i: (b, r, i, 0)),
            pl.BlockSpec((None, n_sub, tq, 2 * LANES), lambda b, r, i: (b, r, i, 0)),
        ],
        out_shape=[
            jax.ShapeDtypeStruct((batch, dilation, m, w), F32),
            jax.ShapeDtypeStruct((batch, dilation, m, 2 * LANES), F32),
        ],
        compiler_params=_params(("parallel", "parallel", "arbitrary")),
        name=f"dilated_attn_{group}",
    )(qkv, qkv, qkv, qkv, qkv, _attn_bias(group))


def _merge_ln_kernel(o0_ref, o1_ref, o2_ref, s0_ref, s1_ref, s2_ref, z_ref, w_ref, x_ref,
                     gate_ref, g_ref, b_ref, out_ref, on_ref, sn_ref, tmp_ref):
    tm = x_ref.shape[0]
    assert DIL_PATTERNS[0][1] == 1
    def scatter(dst, src_ref, cols, dilation, tmp):
        rows = tm // dilation
        if dilation <= SHUFFLE_STRIDE:
            for r in range(dilation):
                dst[pl.ds(r, rows, stride=dilation), :] = src_ref[r, :, cols]
        else:
            assert dilation == SHUFFLE_STRIDE * SHUFFLE_STRIDE
            for r_lo in range(SHUFFLE_STRIDE):
                for r_hi in range(SHUFFLE_STRIDE):
                    tmp[r_lo, pl.ds(r_hi, rows, stride=SHUFFLE_STRIDE), :] = (
                        src_ref[r_lo + SHUFFLE_STRIDE * r_hi, :, cols])
            for r_lo in range(SHUFFLE_STRIDE):
                dst[pl.ds(r_lo, tm // SHUFFLE_STRIDE, stride=SHUFFLE_STRIDE), :] = tmp[r_lo]

    n_tmp = tmp_ref.shape[0]
    for g, (o_ref, s_ref) in ((1, (o1_ref, s1_ref)), (2, (o2_ref, s2_ref))):
        dilation = DIL_PATTERNS[g][1]
        for half in range(2):
            scatter(sn_ref.at[g - 1, half], s_ref, slice(half * LANES, (half + 1) * LANES), dilation,
                    tmp_ref.at[half % n_tmp])
        for h in range(DIL_HEADS):
            scatter(on_ref.at[g - 1, h], o_ref, slice(h * DIL_HEAD_DIM, (h + 1) * DIL_HEAD_DIM), dilation,
                    tmp_ref.at[h % n_tmp])
    l0, l1, l2 = s0_ref[0, :, :LANES], sn_ref[0, 0], sn_ref[1, 0]
    mx = jnp.maximum(jnp.maximum(l0, l1), l2)
    e0, e1, e2 = jnp.exp(l0 - mx), jnp.exp(l1 - mx), jnp.exp(l2 - mx)
    inv = 1.0 / (e0 + e1 + e2)
    c0 = e0 * inv * s0_ref[0, :, LANES:]
    c1 = e1 * inv * sn_ref[0, 1]
    c2 = e2 * inv * sn_ref[1, 1]
    pieces = []
    for h in range(DIL_HEADS):
        pieces.append(o0_ref[0, :, h * DIL_HEAD_DIM:(h + 1) * DIL_HEAD_DIM] * c0[:, h:h + 1]
                      + on_ref[0, h] * c1[:, h:h + 1]
                      + on_ref[1, h] * c2[:, h:h + 1])
    hz = 0.5 * z_ref[...]
    o = jnp.concatenate(pieces, axis=1) * (hz + hz * jnp.tanh(hz))
    y = jnp.dot(o.astype(BF16), w_ref[...], preferred_element_type=F32)
    u = DEEPNORM_ALPHA * x_ref[...] + (1.0 + gate_ref[...]) * y
    out_ref[...] = _layer_norm(u, g_ref[...], b_ref[...])


def _merge_ln(os_, stats, z, w, x2, gate, ln_g, ln_b, *, seq, tm):
    t, d = x2.shape
    ow = DIL_OUT_WIDTH
    per_batch = seq // tm
    row = lambda i: (i, 0)

    def sub_major(width):
        return [pl.BlockSpec((None, dilation, tm // dilation, width),
                             lambda i: (i // per_batch, 0, i % per_batch, 0))
                for _, dilation in DIL_PATTERNS]

    return pl.pallas_call(
        _merge_ln_kernel,
        grid=(t // tm,),
        in_specs=sub_major(ow) + sub_major(2 * LANES) + [
            pl.BlockSpec((tm, ow), row),
            pl.BlockSpec((ow, d), lambda i: (0, 0)),
            pl.BlockSpec((tm, d), row),
            pl.BlockSpec((None, 1, d), lambda i: (i // per_batch, 0, 0)),
            pl.BlockSpec((1, d), lambda i: (0, 0)),
            pl.BlockSpec((1, d), lambda i: (0, 0)),
        ],
        out_specs=pl.BlockSpec((tm, d), row),
        out_shape=jax.ShapeDtypeStruct((t, d), F32),
        scratch_shapes=[pltpu.VMEM((DIL_N_GROUPS - 1, DIL_HEADS, tm, LANES), F32),
                        pltpu.VMEM((DIL_N_GROUPS - 1, 2, tm, LANES), F32),
                        pltpu.VMEM((2, SHUFFLE_STRIDE, tm // SHUFFLE_STRIDE, LANES), F32)],
        compiler_params=_params(("parallel",)),
        name="merge_out_proj_ln",
    )(*os_, *stats, z, w, x2, gate, ln_g.reshape(1, d), ln_b.reshape(1, d))


def _ssd_block(x2, scale, shift, gate, in_w, conv_w, conv_b, dt_bias, a_log, d_skip, norm_g, out_w, ln_g, ln_b,
               *, batch, seq):
    w16 = in_w.astype(BF16)
    dt_w = jnp.pad(w16[:, SSD_D_INNER + SSD_CONV_DIM:], ((0, 0), (0, LANES - SSD_N_HEADS)))
    zx, dt = _in_proj(x2, scale, shift, w16, dt_w, n=SSD_D_INNER + SSD_CONV_DIM, seq=seq, tm=1024, tn=1280)
    y = _ssd_scan(zx, dt, conv_w, conv_b, dt_bias, a_log, d_skip, norm_g, batch=batch, seq=seq)
    return _proj_ln(y, out_w.astype(BF16), x2, gate, ln_g, ln_b, seq=seq, tm=512)


def _dilated_block(x2, scale, shift, gate, kv_w, in_w, out_w, ln_g, ln_b, *, batch, seq):
    w = jnp.concatenate([in_w[:, :DIL_Q_WIDTH], kv_w, in_w[:, DIL_Q_WIDTH:]], axis=1).astype(BF16)
    *qkvs, z = _b_proj(x2, scale, shift, w, batch=batch, seq=seq, tm=512)
    os_, stats = [], []
    for group in range(DIL_N_GROUPS):
        tq = min(ATTN_ROWS, seq // DIL_PATTERNS[group][1])
        o, st = _dilated_attention(qkvs[group], group, n_sub=ATTN_ROWS // tq, tq=tq)
        os_.append(o)
        stats.append(st)
    return _merge_ln(os_, stats, z, out_w.astype(BF16), x2, gate, ln_g, ln_b, seq=seq, tm=512)


def kernel(x, c, ada_w, ada_b, ln_g, ln_b, a_in_w, a_conv_w, a_conv_b, a_dt_bias, a_A_log, a_D,
           a_norm_g, a_out_w, kv_w, b_in_w, b_out_w):
    batch, seq, d = x.shape
    x2 = x.reshape(batch * seq, d)

    c_pad = jnp.pad(c, ((0, 2 * SUBLANES - batch), (0, 0)))
    mod = _adaln(c_pad, ada_w, ada_b)[:, :batch]
    shift = mod[:, :, None, 0:d]
    scale = mod[:, :, None, d:2 * d]
    gate = mod[:, :, None, 2 * d:3 * d]

    x2 = _ssd_block(x2, scale[0], shift[0], gate[0], a_in_w[0], a_conv_w[0], a_conv_b[0], a_dt_bias[0],
                    a_A_log[0], a_D[0], a_norm_g[0], a_out_w[0], ln_g[0], ln_b[0], batch=batch, seq=seq)
    x2 = _dilated_block(x2, scale[1], shift[1], gate[1], kv_w, b_in_w[0], b_out_w[0], ln_g[1], ln_b[1],
                        batch=batch, seq=seq)
    return x2.reshape(batch, seq, d)
```

```python
import numpy as np
import jax
import jax.numpy as jnp
from jax import lax
from jax.experimental import pallas as pl
from jax.experimental.pallas import tpu as pltpu

F32 = jnp.float32
BF16 = jnp.bfloat16

D_MODEL = 2048
DEPTH = 2
LANES = 128
SUBLANES = 8

SSD_D_INNER = 2 * D_MODEL
SSD_HEAD_DIM = 64
SSD_N_HEADS = SSD_D_INNER // SSD_HEAD_DIM
SSD_N_GROUPS = 8
SSD_HEADS_PER_GROUP = SSD_N_HEADS // SSD_N_GROUPS
SSD_D_STATE = 128
SSD_CONV_W = 4
SSD_CHUNK = 256
SSD_BC_DIM = SSD_N_GROUPS * SSD_D_STATE
SSD_CONV_DIM = SSD_D_INNER + 2 * SSD_BC_DIM
SSD_GROUP_W = SSD_HEADS_PER_GROUP * SSD_HEAD_DIM

DIL_PATTERNS = ((128, 1), (512, 4), (2048, 16))
DIL_N_GROUPS = len(DIL_PATTERNS)
DIL_HEADS = 8
DIL_HEAD_DIM = 128
DIL_Q_WIDTH = DIL_N_GROUPS * DIL_HEADS * DIL_HEAD_DIM
DIL_OUT_WIDTH = DIL_HEADS * DIL_HEAD_DIM
DIL_BLOCK = 128
SHUFFLE_STRIDE = 4
ATTN_ROWS = 1024

DEEPNORM_ALPHA = (2 * DEPTH) ** 0.25
LN_EPS = 1e-5
RMS_EPS = 1e-5
LOG2E = 1.4426950408889634

VMEM_LIMIT = 56 * 1024 * 1024


def _params(sem):
    return pltpu.CompilerParams(dimension_semantics=sem, vmem_limit_bytes=VMEM_LIMIT)


def _silu(v):
    return v * (1.0 / (1.0 + jnp.exp(-v)))


def _split3(v):
    v1 = v.astype(BF16)
    r1 = v - v1.astype(F32)
    v2 = r1.astype(BF16)
    r2 = r1 - v2.astype(F32)
    v3 = r2.astype(BF16)
    return jnp.concatenate([v1, v2, v3], axis=1)


def _layer_norm(u, g, b):
    mu = jnp.mean(u, axis=-1, keepdims=True)
    d = u - mu
    var = jnp.mean(d * d, axis=-1, keepdims=True)
    return d * lax.rsqrt(var + LN_EPS) * g + b


def _adaln_kernel(c_ref, w_ref, b_ref, o_ref):
    s = _silu(c_ref[...]).astype(BF16)
    o_ref[...] = jnp.dot(s, w_ref[...].astype(BF16), preferred_element_type=F32) + b_ref[...]


def _adaln(c_pad, ada_w, ada_b):
    rows = c_pad.shape[0]
    n = 3 * D_MODEL
    tn = 768
    return pl.pallas_call(
        _adaln_kernel,
        grid=(DEPTH, n // tn),
        in_specs=[
            pl.BlockSpec((rows, D_MODEL), lambda l, j: (0, 0)),
            pl.BlockSpec((None, D_MODEL, tn), lambda l, j: (l, 0, j)),
            pl.BlockSpec((None, 1, tn), lambda l, j: (l, 0, j)),
        ],
        out_specs=pl.BlockSpec((None, rows, tn), lambda l, j: (l, 0, j)),
        out_shape=jax.ShapeDtypeStruct((DEPTH, rows, n), F32),
        compiler_params=_params(("parallel", "parallel")),
        name="adaln",
    )(c_pad, ada_w, ada_b.reshape(DEPTH, 1, n))


def _in_proj_kernel(x_ref, scale_ref, shift_ref, w_ref, wdt_ref, o_ref, dt_ref, h_ref):
    @pl.when(pl.program_id(1) == 0)
    def _():
        h = (x_ref[...] * (1.0 + scale_ref[...]) + shift_ref[...]).astype(BF16)
        h_ref[...] = h
        dt_ref[...] = jnp.dot(h, wdt_ref[...], preferred_element_type=F32)

    o_ref[...] = jnp.dot(h_ref[...], w_ref[...], preferred_element_type=F32)


def _in_proj(x2, scale, shift, w, wdt, *, n, seq, tm, tn):
    t, d = x2.shape
    per_batch = seq // tm
    return pl.pallas_call(
        _in_proj_kernel,
        grid=(t // tm, n // tn),
        in_specs=[
            pl.BlockSpec((tm, d), lambda i, j: (i, 0)),
            pl.BlockSpec((None, 1, d), lambda i, j: (i // per_batch, 0, 0)),
            pl.BlockSpec((None, 1, d), lambda i, j: (i // per_batch, 0, 0)),
            pl.BlockSpec((d, tn), lambda i, j: (0, j)),
            pl.BlockSpec((d, LANES), lambda i, j: (0, 0)),
        ],
        out_specs=[pl.BlockSpec((tm, tn), lambda i, j: (i, j)), pl.BlockSpec((tm, LANES), lambda i, j: (i, 0))],
        out_shape=[jax.ShapeDtypeStruct((t, n), F32), jax.ShapeDtypeStruct((t, LANES), F32)],
        scratch_shapes=[pltpu.VMEM((tm, d), BF16)],
        compiler_params=_params(("parallel", "arbitrary")),
        name="ssd_in_proj",
    )(x2, scale, shift, w, wdt)


def _conv_silu(raw, carry, w, b):
    assert SSD_CONV_W == 4
    wh = 0.5 * w
    row = lax.broadcasted_iota(jnp.int32, carry.shape, 0)

    def delayed(cur, tail, s):
        sh = pltpu.roll(cur, s, axis=0)
        prev = pltpu.roll(tail, s, axis=0)
        head = jnp.where(row < s, prev, sh[:SUBLANES])
        return jnp.concatenate([head, sh[SUBLANES:]], axis=0)

    x1 = delayed(raw, carry, 1)
    near = raw * wh[3:4, :] + x1 * wh[2:3, :] + 0.5 * b
    far = raw * wh[1:2, :] + x1 * wh[0:1, :]
    far_tail = carry * wh[1:2, :] + pltpu.roll(carry, 1, axis=0) * wh[0:1, :]
    acc = near + delayed(far, far_tail, 2)
    return acc + acc * jnp.tanh(acc)


def _ssd_kernel(z_ref, x_ref, b_ref, c_ref, dt_ref, cw_ref, cb_ref, dtb_ref, alog_ref, dexp_ref, ng_ref,
                tril_ref, e_ref, sel_ref,
                y_ref,
                state_ref, carx_ref, carb_ref, carc_ref, v3_ref, ac3_ref, ybuf_ref):
    L = SSD_CHUNK
    gw = SSD_GROUP_W
    ns = SSD_D_STATE

    @pl.when(pl.program_id(1) == 0)
    def _():
        state_ref[...] = jnp.zeros(state_ref.shape, F32)
        carx_ref[...] = jnp.zeros(carx_ref.shape, F32)
        carb_ref[...] = jnp.zeros(carb_ref.shape, F32)
        carc_ref[...] = jnp.zeros(carc_ref.shape, F32)

    v = dt_ref[...] + dtb_ref[...]
    dt = jnp.maximum(v, 0.0) + jnp.log1p(jnp.exp(-jnp.abs(v)))
    a = dt * (-jnp.exp(alog_ref[...]))
    a3 = _split3(a)
    tril = tril_ref[...]
    acum = (jnp.dot(tril, a3[:, :LANES], preferred_element_type=F32)
            + jnp.dot(tril, a3[:, LANES:2 * LANES], preferred_element_type=F32)
            + jnp.dot(tril, a3[:, 2 * LANES:], preferred_element_type=F32))
    tail = jnp.exp(acum[L - 1:L, :] - acum)
    v3_ref[0:L, :] = _split3(dt)
    v3_ref[L:2 * L, :] = _split3(tail)
    v3_ref[2 * L:3 * L, :] = _split3(jnp.exp(acum))
    ac3_ref[...] = _split3(acum)

    li = lax.broadcasted_iota(jnp.int32, (L, L), 0)
    si = lax.broadcasted_iota(jnp.int32, (L, L), 1)
    causal = li >= si
    lane = lax.broadcasted_iota(jnp.int32, (L, LANES), 1)
    heads_per_tile = LANES // SSD_HEAD_DIM
    assert heads_per_tile == 2
    n_pairs = gw // LANES

    for g in range(SSD_N_GROUPS):
        xs = slice(g * gw, (g + 1) * gw)
        bs = slice(g * ns, (g + 1) * ns)
        wb = slice(SSD_D_INNER + g * ns, SSD_D_INNER + (g + 1) * ns)
        wc = slice(SSD_D_INNER + SSD_BC_DIM + g * ns, SSD_D_INNER + SSD_BC_DIM + (g + 1) * ns)

        ex = jnp.dot(v3_ref[...], e_ref[:, xs], preferred_element_type=F32)
        dt_e = ex[0:L]
        tail_e = ex[L:2 * L]
        eac_e = ex[2 * L:3 * L]
        acg = jnp.dot(ac3_ref[...], sel_ref[:, bs], preferred_element_type=F32) * LOG2E
        acg_t = acg.T

        b_raw = b_ref[:, bs]
        c_raw = c_ref[:, bs]
        bg = _conv_silu(b_raw, carb_ref[:, bs], cw_ref[:, wb], cb_ref[:, wb])
        cg = _conv_silu(c_raw, carc_ref[:, bs], cw_ref[:, wc], cb_ref[:, wc])
        carb_ref[:, bs] = b_raw[L - SUBLANES:, :]
        carc_ref[:, bs] = c_raw[L - SUBLANES:, :]
        cb16 = cg.astype(BF16)
        bb16 = bg.astype(BF16)
        cb = lax.dot_general(cb16, bb16, (((1,), (1,)), ((), ())), preferred_element_type=F32)
        cb = jnp.where(causal, cb, 0.0)

        state = state_ref[g]
        y_state = jnp.dot(cb16, state.astype(BF16), preferred_element_type=F32)

        ssq = jnp.zeros((L, LANES), F32)
        for pair in range(n_pairs):
            ps = slice(pair * LANES, (pair + 1) * LANES)
            cs = slice(g * gw + pair * LANES, g * gw + (pair + 1) * LANES)
            x_raw = x_ref[:, cs]
            xg = _conv_silu(x_raw, carx_ref[:, cs], cw_ref[:, cs], cb_ref[:, cs])
            carx_ref[:, cs] = x_raw[L - SUBLANES:, :]
            xdt = xg * dt_e[:, ps]
            ms = []
            for sub in range(heads_per_tile):
                k = pair * heads_per_tile + sub
                seg = acg[:, k:k + 1] - acg_t[k:k + 1, :]
                ms.append((cb * jnp.exp2(jnp.minimum(seg, 0.0))).astype(BF16))
            both = jnp.dot(jnp.concatenate(ms, axis=0), xdt.astype(BF16), preferred_element_type=F32)
            y = (jnp.where(lane < SSD_HEAD_DIM, both[:L], both[L:]) + y_state[:, ps] * eac_e[:, ps]
                 + xg * dexp_ref[:, cs])
            xtail = (xdt * tail_e[:, ps]).astype(BF16)
            upd = lax.dot_general(bb16, xtail, (((0,), (0,)), ((), ())), preferred_element_type=F32)
            state_ref[g, :, ps] = state[:, ps] * eac_e[L - 1:L, ps] + upd
            hz = 0.5 * z_ref[:, cs]
            y = y * (hz + hz * jnp.tanh(hz))
            ssq = ssq + y * y
            ybuf_ref[:, ps] = y
        r = lax.rsqrt(jnp.sum(ssq, axis=-1, keepdims=True) * (1.0 / gw) + RMS_EPS)
        y_ref[:, xs] = (ybuf_ref[...] * r * ng_ref[:, xs]).astype(y_ref.dtype)


def _ssd_constants():
    tril = np.tril(np.ones((SSD_CHUNK, SSD_CHUNK), np.float32))
    head_of_col = np.arange(SSD_D_INNER) // SSD_HEAD_DIM
    e = (np.arange(LANES)[:, None] == head_of_col[None, :]).astype(np.float32)
    col = np.arange(SSD_N_GROUPS * LANES)
    src = np.where(col % LANES < SSD_HEADS_PER_GROUP,
                   (col // LANES) * SSD_HEADS_PER_GROUP + col % LANES, -1)
    sel = (np.arange(LANES)[:, None] == src[None, :]).astype(np.float32)
    e3 = np.concatenate([e, e, e], axis=0)
    sel3 = np.concatenate([sel, sel, sel], axis=0)
    return jnp.asarray(tril, BF16), jnp.asarray(e3, BF16), jnp.asarray(sel3, BF16)


def _ssd_scan(zx, dt, conv_w, conv_b, dt_bias, a_log, d_skip, norm_g, *, batch, seq):
    t = zx.shape[0]
    L = SSD_CHUNK
    nc = seq // L
    G = SSD_N_GROUPS
    gw = SSD_GROUP_W
    ns = SSD_D_STATE
    tril, e3, sel3 = _ssd_constants()
    pad = LANES - SSD_N_HEADS
    dtb = jnp.pad(dt_bias, (0, pad)).reshape(1, LANES)
    alog = jnp.pad(a_log, (0, pad)).reshape(1, LANES)
    dexp = jnp.repeat(d_skip, SSD_HEAD_DIM).reshape(1, SSD_D_INNER)
    ng = norm_g.reshape(1, SSD_D_INNER)
    cb2 = conv_b.reshape(1, SSD_CONV_DIM)

    zw = SSD_D_INNER
    const = lambda b, c: (0, 0)
    in_specs = [
        pl.BlockSpec((L, zw), lambda b, c: (b * nc + c, 0)),
        pl.BlockSpec((L, zw), lambda b, c: (b * nc + c, 1)),
        pl.BlockSpec((L, SSD_BC_DIM), lambda b, c: (b * nc + c, 2 * zw // SSD_BC_DIM)),
        pl.BlockSpec((L, SSD_BC_DIM), lambda b, c: (b * nc + c, 2 * zw // SSD_BC_DIM + 1)),
        pl.BlockSpec((L, LANES), lambda b, c: (b * nc + c, 0)),
        pl.BlockSpec((SSD_CONV_W, SSD_CONV_DIM), const),
        pl.BlockSpec((1, SSD_CONV_DIM), const),
        pl.BlockSpec((1, LANES), const),
        pl.BlockSpec((1, LANES), const),
        pl.BlockSpec((1, zw), const),
        pl.BlockSpec((1, zw), const),
        pl.BlockSpec((L, L), const),
        pl.BlockSpec((3 * LANES, zw), const),
        pl.BlockSpec((3 * LANES, G * LANES), const),
    ]
    return pl.pallas_call(
        _ssd_kernel,
        grid=(batch, nc),
        in_specs=in_specs,
        out_specs=pl.BlockSpec((L, zw), lambda b, c: (b * nc + c, 0)),
        out_shape=jax.ShapeDtypeStruct((t, zw), BF16),
        scratch_shapes=[
            pltpu.VMEM((G, ns, gw), F32),
            pltpu.VMEM((SUBLANES, zw), F32),
            pltpu.VMEM((SUBLANES, SSD_BC_DIM), F32),
            pltpu.VMEM((SUBLANES, SSD_BC_DIM), F32),
            pltpu.VMEM((3 * L, 3 * LANES), BF16),
            pltpu.VMEM((L, 3 * LANES), BF16),
            pltpu.VMEM((L, gw), F32),
        ],
        compiler_params=_params(("parallel", "arbitrary")),
        name="ssd_scan",
    )(zx, zx, zx, zx, dt, conv_w, cb2, dtb, alog, dexp, ng, tril, e3, sel3)


def _proj_ln_kernel(y_ref, w_ref, x_ref, gate_ref, g_ref, b_ref, o_ref):
    acc = jnp.dot(y_ref[...], w_ref[...], preferred_element_type=F32)
    u = DEEPNORM_ALPHA * x_ref[...] + (1.0 + gate_ref[...]) * acc
    o_ref[...] = _layer_norm(u, g_ref[...], b_ref[...])


def _proj_ln(y, w, x2, gate, ln_g, ln_b, *, seq, tm):
    t, kdim = y.shape
    d = w.shape[1]
    per_batch = seq // tm
    return pl.pallas_call(
        _proj_ln_kernel,
        grid=(t // tm,),
        in_specs=[
            pl.BlockSpec((tm, kdim), lambda i: (i, 0)),
            pl.BlockSpec((kdim, d), lambda i: (0, 0), pipeline_mode=pl.Buffered(1)),
            pl.BlockSpec((tm, d), lambda i: (i, 0)),
            pl.BlockSpec((None, 1, d), lambda i: (i // per_batch, 0, 0)),
            pl.BlockSpec((1, d), lambda i: (0, 0)),
            pl.BlockSpec((1, d), lambda i: (0, 0)),
        ],
        out_specs=pl.BlockSpec((tm, d), lambda i: (i, 0)),
        out_shape=jax.ShapeDtypeStruct((t, d), F32),
        compiler_params=_params(("parallel",)),
        name="out_proj_ln",
    )(y, w, x2, gate, ln_g.reshape(1, d), ln_b.reshape(1, d))


def _b_proj_kernel(x_ref, scale_ref, shift_ref, w_ref, o0_ref, o1_ref, o2_ref, h_ref, res_ref, tmp_ref):
    tm = x_ref.shape[0]
    n_qkv = 3 * DIL_N_GROUPS
    ow = DIL_OUT_WIDTH
    block_tiles = ow // LANES
    tiles = 2
    n_chunks = w_ref.shape[1] // (tiles * LANES)
    step_blocks = w_ref.shape[1] // ow
    out_refs = (o0_ref, o1_ref, o2_ref)

    @pl.when(pl.program_id(1) == 0)
    def _():
        x = x_ref[...]
        h_ref[0] = (x * (1.0 + scale_ref[...]) + shift_ref[...]).astype(BF16)
        h_ref[1] = x.astype(BF16)

    def emit_tile(block, gt):
        cols = slice((gt % block_tiles) * LANES, (gt % block_tiles + 1) * LANES)
        out_ref = out_refs[block % DIL_N_GROUPS]
        dilation = DIL_PATTERNS[block % DIL_N_GROUPS][1]
        rows = tm // dilation
        if dilation <= SHUFFLE_STRIDE:
            for r in range(dilation):
                out_ref[r, :, cols] = res_ref[gt, pl.ds(r, rows, stride=dilation), :].astype(BF16)
        else:
            assert dilation == SHUFFLE_STRIDE * SHUFFLE_STRIDE
            tmp = tmp_ref.at[gt % 2]
            for r_lo in range(SHUFFLE_STRIDE):
                tmp[r_lo] = res_ref[gt, pl.ds(r_lo, tm // SHUFFLE_STRIDE, stride=SHUFFLE_STRIDE), :]
            for r_lo in range(SHUFFLE_STRIDE):
                for r_hi in range(SHUFFLE_STRIDE):
                    out_ref[r_lo + SHUFFLE_STRIDE * r_hi, :, cols] = (
                        tmp[r_lo, pl.ds(r_hi, rows, stride=SHUFFLE_STRIDE), :].astype(BF16))

    def step(jj):
        def block_of(gt):
            return jj * step_blocks + gt // block_tiles

        def matmul_chunk(c):
            block = block_of(c * tiles)
            plain = 1 if DIL_N_GROUPS <= block < n_qkv else 0
            cols = slice(c * tiles * LANES, (c + 1) * tiles * LANES)
            acc = jnp.dot(h_ref[plain], w_ref[:, cols], preferred_element_type=F32)
            for t in range(tiles):
                res_ref[c * tiles + t] = acc[:, t * LANES:(t + 1) * LANES]

        matmul_chunk(0)
        for c in range(1, n_chunks + 1):
            if c < n_chunks:
                matmul_chunk(c)
            for gt in range((c - 1) * tiles, c * tiles):
                emit_tile(block_of(gt), gt)

    for jj in range(n_qkv // step_blocks):
        pl.when(pl.program_id(1) == jj)(lambda jj=jj: step(jj))


def _b_proj(x2, scale, shift, w, *, batch, seq, tm):
    t, d = x2.shape
    ow = DIL_OUT_WIDTH
    per_batch = seq // tm
    step_blocks = DIL_N_GROUPS
    assert w.shape[1] == 3 * DIL_N_GROUPS * ow
    out_specs, out_shape = [], []
    for _, dilation in DIL_PATTERNS:
        out_specs.append(pl.BlockSpec(
            (None, dilation, tm // dilation, ow), lambda i, j: (i // per_batch, 0, i % per_batch, j)))
        out_shape.append(jax.ShapeDtypeStruct((batch, dilation, seq // dilation, 3 * ow), BF16))
    return pl.pallas_call(
        _b_proj_kernel,
        grid=(t // tm, w.shape[1] // (step_blocks * ow)),
        in_specs=[
            pl.BlockSpec((tm, d), lambda i, j: (i, 0)),
            pl.BlockSpec((None, 1, d), lambda i, j: (i // per_batch, 0, 0)),
            pl.BlockSpec((None, 1, d), lambda i, j: (i // per_batch, 0, 0)),
            pl.BlockSpec((d, step_blocks * ow), lambda i, j: (0, j)),
        ],
        out_specs=out_specs,
        out_shape=out_shape,
        scratch_shapes=[pltpu.VMEM((2, tm, d), BF16), pltpu.VMEM((step_blocks * ow // LANES, tm, LANES), F32),
                        pltpu.VMEM((2, SHUFFLE_STRIDE, tm // SHUFFLE_STRIDE, LANES), F32)],
        compiler_params=_params(("parallel", "arbitrary")),
        name="dilated_in_proj",
    )(x2, scale, shift, w)


def _gate_proj_kernel(x_ref, scale_ref, shift_ref, w_ref, o_ref):
    h = (x_ref[...] * (1.0 + scale_ref[...]) + shift_ref[...]).astype(BF16)
    o_ref[...] = jnp.dot(h, w_ref[...], preferred_element_type=F32)


def _gate_proj(x2, scale, shift, w, *, seq, tm):
    t, d = x2.shape
    n = w.shape[1]
    per_batch = seq // tm
    return pl.pallas_call(
        _gate_proj_kernel,
        grid=(t // tm,),
        in_specs=[
            pl.BlockSpec((tm, d), lambda i: (i, 0)),
            pl.BlockSpec((None, 1, d), lambda i: (i // per_batch, 0, 0)),
            pl.BlockSpec((None, 1, d), lambda i: (i // per_batch, 0, 0)),
            pl.BlockSpec((d, n), lambda i: (0, 0)),
        ],
        out_specs=pl.BlockSpec((tm, n), lambda i: (i, 0)),
        out_shape=jax.ShapeDtypeStruct((t, n), F32),
        compiler_params=_params(("parallel",)),
        name="dilated_gate_proj",
    )(x2, scale, shift, w)


def _attn_bias(group):
    _, dilation = DIL_PATTERNS[group]
    n_all = DIL_N_GROUPS * DIL_HEADS
    slopes = 2.0 ** (-8.0 * np.arange(1, n_all + 1) / n_all)
    slopes = slopes.reshape(DIL_N_GROUPS, DIL_HEADS)[group].astype(np.float32)
    qi = np.arange(DIL_BLOCK)[:, None]
    kj = np.arange(2 * DIL_BLOCK)[None, :]
    delta = qi + DIL_BLOCK - kj
    valid = (delta >= 0) & (delta <= DIL_BLOCK)
    alibi = -slopes[:, None, None] * (delta * dilation).astype(np.float32)[None]
    return jnp.asarray(np.where(valid[None], alibi * np.float32(LOG2E), -np.inf).astype(np.float32))


def _attn_kernel(q_ref, k_ref, v_ref, kp_ref, vp_ref, bias_ref, o_ref, st_ref):
    first = pl.program_id(2) == 0
    blk = DIL_BLOCK
    n_sub, tq = q_ref.shape[0], q_ref.shape[1]
    scale = DIL_HEAD_DIM ** -0.5 * LOG2E
    dn = (((1,), (1,)), ((), ()))
    lane = lax.broadcasted_iota(jnp.int32, (blk, LANES), 1)
    kcol = lax.broadcasted_iota(jnp.int32, (blk, 2 * blk), 1)
    no_prev = first & (kcol < blk)
    for r in range(n_sub):
        for sb in range(tq // blk):
            rows = slice(sb * blk, (sb + 1) * blk)
            keys = slice((sb - 1) * blk, (sb + 1) * blk)
            m_tile = jnp.zeros((blk, LANES), F32)
            den_tile = jnp.ones((blk, LANES), F32)
            ss = []
            for h in range(DIL_HEADS):
                cols = slice(h * DIL_HEAD_DIM, (h + 1) * DIL_HEAD_DIM)
                q = q_ref[r, rows, cols]
                if sb == 0:
                    s = jnp.concatenate(
                        [lax.dot_general(q, kp_ref[r, :, cols], dn, preferred_element_type=F32),
                         lax.dot_general(q, k_ref[r, rows, cols], dn, preferred_element_type=F32)], axis=1)
                else:
                    s = lax.dot_general(q, k_ref[r, keys, cols], dn, preferred_element_type=F32)
                ss.append(s)
            ps = []
            for h in range(DIL_HEADS):
                bias = bias_ref[h]
                if sb == 0:
                    bias = jnp.where(no_prev, -jnp.inf, bias)
                s = ss[h] * scale + bias
                m = jnp.max(s, axis=-1, keepdims=True)
                p = jnp.exp2(s - m)
                den = jnp.sum(p, axis=-1, keepdims=True)
                ps.append(p.astype(BF16))
                m_tile = jnp.where(lane == h, m, m_tile)
                den_tile = jnp.where(lane == h, den, den_tile)
            for h in range(DIL_HEADS):
                cols = slice(h * DIL_HEAD_DIM, (h + 1) * DIL_HEAD_DIM)
                p16 = ps[h]
                if sb == 0:
                    o = (jnp.dot(p16[:, :blk], vp_ref[r, :, cols], preferred_element_type=F32)
                         + jnp.dot(p16[:, blk:], v_ref[r, rows, cols], preferred_element_type=F32))
                else:
                    o = jnp.dot(p16, v_ref[r, keys, cols], preferred_element_type=F32)
                o_ref[r, rows, cols] = o
            st_ref[r, rows, :LANES] = (m_tile + jnp.log2(den_tile)) * (1.0 / LOG2E)
            st_ref[r, rows, LANES:] = 1.0 / den_tile


def _dilated_attention(qkv, group, *, n_sub, tq):
    batch, dilation, m, _ = qkv.shape
    assert DIL_PATTERNS[group][0] // dilation == DIL_BLOCK and m % tq == 0 and dilation % n_sub == 0
    w = DIL_OUT_WIDTH
    blk = DIL_BLOCK
    sub = tq // blk

    def prev(i):
        return jnp.maximum(i * sub - 1, 0)

    return pl.pallas_call(
        _attn_kernel,
        grid=(batch, dilation // n_sub, m // tq),
        in_specs=[
            pl.BlockSpec((None, n_sub, tq, w), lambda b, r, i: (b, r, i, 0)),
            pl.BlockSpec((None, n_sub, tq, w), lambda b, r, i: (b, r, i, 1)),
            pl.BlockSpec((None, n_sub, tq, w), lambda b, r, i: (b, r, i, 2)),
            pl.BlockSpec((None, n_sub, blk, w), lambda b, r, i: (b, r, prev(i), 1)),
            pl.BlockSpec((None, n_sub, blk, w), lambda b, r, i: (b, r, prev(i), 2)),
            pl.BlockSpec((DIL_HEADS, blk, 2 * blk), lambda b, r, i: (0, 0, 0)),
        ],
        out_specs=[
            pl.BlockSpec((None, n_sub, tq, w), lambda b, r, i: (b, r, i, 0)),
            pl.BlockSpec((None, n_sub, tq, 2 * LANES), lambda b, r, i: (b, r, i, 0)),
        ],
        out_shape=[
            jax.ShapeDtypeStruct((batch, dilation, m, w), F32),
            jax.ShapeDtypeStruct((batch, dilation, m, 2 * LANES), F32),
        ],
        compiler_params=_params(("parallel", "parallel", "arbitrary")),
        name=f"dilated_attn_{group}",
    )(qkv, qkv, qkv, qkv, qkv, _attn_bias(group))


def _merge_ln_kernel(o0_ref, o1_ref, o2_ref, s0_ref, s1_ref, s2_ref, z_ref, w_ref, x_ref,
                     gate_ref, g_ref, b_ref, out_ref, on_ref, sn_ref, tmp_ref):
    tm = x_ref.shape[0]
    assert DIL_PATTERNS[0][1] == 1
    def scatter(dst, src_ref, cols, dilation, tmp):
        rows = tm // dilation
        if dilation <= SHUFFLE_STRIDE:
            for r in range(dilation):
                dst[pl.ds(r, rows, stride=dilation), :] = src_ref[r, :, cols]
        else:
            assert dilation == SHUFFLE_STRIDE * SHUFFLE_STRIDE
            for r_lo in range(SHUFFLE_STRIDE):
                for r_hi in range(SHUFFLE_STRIDE):
                    tmp[r_lo, pl.ds(r_hi, rows, stride=SHUFFLE_STRIDE), :] = (
                        src_ref[r_lo + SHUFFLE_STRIDE * r_hi, :, cols])
            for r_lo in range(SHUFFLE_STRIDE):
                dst[pl.ds(r_lo, tm // SHUFFLE_STRIDE, stride=SHUFFLE_STRIDE), :] = tmp[r_lo]

    n_tmp = tmp_ref.shape[0]
    for g, (o_ref, s_ref) in ((1, (o1_ref, s1_ref)), (2, (o2_ref, s2_ref))):
        dilation = DIL_PATTERNS[g][1]
        for half in range(2):
            scatter(sn_ref.at[g - 1, half], s_ref, slice(half * LANES, (half + 1) * LANES), dilation,
                    tmp_ref.at[half % n_tmp])
        for h in range(DIL_HEADS):
            scatter(on_ref.at[g - 1, h], o_ref, slice(h * DIL_HEAD_DIM, (h + 1) * DIL_HEAD_DIM), dilation,
                    tmp_ref.at[h % n_tmp])
    l0, l1, l2 = s0_ref[0, :, :LANES], sn_ref[0, 0], sn_ref[1, 0]
    mx = jnp.maximum(jnp.maximum(l0, l1), l2)
    e0, e1, e2 = jnp.exp(l0 - mx), jnp.exp(l1 - mx), jnp.exp(l2 - mx)
    inv = 1.0 / (e0 + e1 + e2)
    c0 = e0 * inv * s0_ref[0, :, LANES:]
    c1 = e1 * inv * sn_ref[0, 1]
    c2 = e2 * inv * sn_ref[1, 1]
    pieces = []
    for h in range(DIL_HEADS):
        pieces.append(o0_ref[0, :, h * DIL_HEAD_DIM:(h + 1) * DIL_HEAD_DIM] * c0[:, h:h + 1]
                      + on_ref[0, h] * c1[:, h:h + 1]
                      + on_ref[1, h] * c2[:, h:h + 1])
    hz = 0.5 * z_ref[...]
    o = jnp.concatenate(pieces, axis=1) * (hz + hz * jnp.tanh(hz))
    y = jnp.dot(o.astype(BF16), w_ref[...], preferred_element_type=F32)
    u = DEEPNORM_ALPHA * x_ref[...] + (1.0 + gate_ref[...]) * y
    out_ref[...] = _layer_norm(u, g_ref[...], b_ref[...])


def _merge_ln(os_, stats, z, w, x2, gate, ln_g, ln_b, *, seq, tm):
    t, d = x2.shape
    ow = DIL_OUT_WIDTH
    per_batch = seq // tm
    row = lambda i: (i, 0)

    def sub_major(width):
        return [pl.BlockSpec((None, dilation, tm // dilation, width),
                             lambda i: (i // per_batch, 0, i % per_batch, 0))
                for _, dilation in DIL_PATTERNS]

    return pl.pallas_call(
        _merge_ln_kernel,
        grid=(t // tm,),
        in_specs=sub_major(ow) + sub_major(2 * LANES) + [
            pl.BlockSpec((tm, ow), row),
            pl.BlockSpec((ow, d), lambda i: (0, 0)),
            pl.BlockSpec((tm, d), row),
            pl.BlockSpec((None, 1, d), lambda i: (i // per_batch, 0, 0)),
            pl.BlockSpec((1, d), lambda i: (0, 0)),
            pl.BlockSpec((1, d), lambda i: (0, 0)),
        ],
        out_specs=pl.BlockSpec((tm, d), row),
        out_shape=jax.ShapeDtypeStruct((t, d), F32),
        scratch_shapes=[pltpu.VMEM((DIL_N_GROUPS - 1, DIL_HEADS, tm, LANES), F32),
                        pltpu.VMEM((DIL_N_GROUPS - 1, 2, tm, LANES), F32),
                        pltpu.VMEM((2, SHUFFLE_STRIDE, tm // SHUFFLE_STRIDE, LANES), F32)],
        compiler_params=_params(("parallel",)),
        name="merge_out_proj_ln",
    )(*os_, *stats, z, w, x2, gate, ln_g.reshape(1, d), ln_b.reshape(1, d))


def _ssd_block(x2, scale, shift, gate, in_w, conv_w, conv_b, dt_bias, a_log, d_skip, norm_g, out_w, ln_g, ln_b,
               *, batch, seq):
    w16 = in_w.astype(BF16)
    dt_w = jnp.pad(w16[:, SSD_D_INNER + SSD_CONV_DIM:], ((0, 0), (0, LANES - SSD_N_HEADS)))
    zx, dt = _in_proj(x2, scale, shift, w16, dt_w, n=SSD_D_INNER + SSD_CONV_DIM, seq=seq, tm=1024, tn=1280)
    y = _ssd_scan(zx, dt, conv_w, conv_b, dt_bias, a_log, d_skip, norm_g, batch=batch, seq=seq)
    return _proj_ln(y, out_w.astype(BF16), x2, gate, ln_g, ln_b, seq=seq, tm=512)


def _dilated_block(x2, scale, shift, gate, kv_w, in_w, out_w, ln_g, ln_b, *, batch, seq):
    w = jnp.concatenate([in_w[:, :DIL_Q_WIDTH], kv_w], axis=1).astype(BF16)
    qkvs = _b_proj(x2, scale, shift, w, batch=batch, seq=seq, tm=512)
    z = _gate_proj(x2, scale, shift, in_w[:, DIL_Q_WIDTH:].astype(BF16), seq=seq, tm=1024)
    os_, stats = [], []
    for group in range(DIL_N_GROUPS):
        tq = min(ATTN_ROWS, seq // DIL_PATTERNS[group][1])
        o, st = _dilated_attention(qkvs[group], group, n_sub=ATTN_ROWS // tq, tq=tq)
        os_.append(o)
        stats.append(st)
    return _merge_ln(os_, stats, z, out_w.astype(BF16), x2, gate, ln_g, ln_b, seq=seq, tm=512)


def kernel(x, c, ada_w, ada_b, ln_g, ln_b, a_in_w, a_conv_w, a_conv_b, a_dt_bias, a_A_log, a_D,
           a_norm_g, a_out_w, kv_w, b_in_w, b_out_w):
    batch, seq, d = x.shape
    x2 = x.reshape(batch * seq, d)

    c_pad = jnp.pad(c, ((0, 2 * SUBLANES - batch), (0, 0)))
    mod = _adaln(c_pad, ada_w, ada_b)[:, :batch]
    shift = mod[:, :, None, 0:d]
    scale = mod[:, :, None, d:2 * d]
    gate = mod[:, :, None, 2 * d:3 * d]

    x2 = _ssd_block(x2, scale[0], shift[0], gate[0], a_in_w[0], a_conv_w[0], a_conv_b[0], a_dt_bias[0],
                    a_A_log[0], a_D[0], a_norm_g[0], a_out_w[0], ln_g[0], ln_b[0], batch=batch, seq=seq)
    x2 = _dilated_block(x2, scale[1], shift[1], gate[1], kv_w, b_in_w[0], b_out_w[0], ln_g[1], ln_b[1],
                        batch=batch, seq=seq)
    return x2.reshape(batch, seq, d)
```

```python
import numpy as np
import jax
import jax.numpy as jnp
from jax import lax
from jax.experimental import pallas as pl
from jax.experimental.pallas import tpu as pltpu

F32 = jnp.float32
BF16 = jnp.bfloat16

D_MODEL = 2048
DEPTH = 2
LANES = 128
SUBLANES = 8

SSD_D_INNER = 2 * D_MODEL
SSD_HEAD_DIM = 64
SSD_N_HEADS = SSD_D_INNER // SSD_HEAD_DIM
SSD_N_GROUPS = 8
SSD_HEADS_PER_GROUP = SSD_N_HEADS // SSD_N_GROUPS
SSD_D_STATE = 128
SSD_CONV_W = 4
SSD_CHUNK = 256
SSD_BC_DIM = SSD_N_GROUPS * SSD_D_STATE
SSD_CONV_DIM = SSD_D_INNER + 2 * SSD_BC_DIM
SSD_GROUP_W = SSD_HEADS_PER_GROUP * SSD_HEAD_DIM

DIL_PATTERNS = ((128, 1), (512, 4), (2048, 16))
DIL_N_GROUPS = len(DIL_PATTERNS)
DIL_HEADS = 8
DIL_HEAD_DIM = 128
DIL_Q_WIDTH = DIL_N_GROUPS * DIL_HEADS * DIL_HEAD_DIM
DIL_OUT_WIDTH = DIL_HEADS * DIL_HEAD_DIM
DIL_BLOCK = 128
SHUFFLE_STRIDE = 4
ATTN_ROWS = 1024

DEEPNORM_ALPHA = (2 * DEPTH) ** 0.25
LN_EPS = 1e-5
RMS_EPS = 1e-5
LOG2E = 1.4426950408889634

VMEM_LIMIT = 56 * 1024 * 1024


def _params(sem):
    return pltpu.CompilerParams(dimension_semantics=sem, vmem_limit_bytes=VMEM_LIMIT)


def _silu(v):
    return v * (1.0 / (1.0 + jnp.exp(-v)))


def _split3(v):
    v1 = v.astype(BF16)
    r1 = v - v1.astype(F32)
    v2 = r1.astype(BF16)
    r2 = r1 - v2.astype(F32)
    v3 = r2.astype(BF16)
    return jnp.concatenate([v1, v2, v3], axis=1)


def _layer_norm(u, g, b):
    mu = jnp.mean(u, axis=-1, keepdims=True)
    d = u - mu
    var = jnp.mean(d * d, axis=-1, keepdims=True)
    return d * lax.rsqrt(var + LN_EPS) * g + b


def _adaln_kernel(c_ref, w_ref, b_ref, o_ref):
    s = _silu(c_ref[...]).astype(BF16)
    o_ref[...] = jnp.dot(s, w_ref[...].astype(BF16), preferred_element_type=F32) + b_ref[...]


def _adaln(c_pad, ada_w, ada_b):
    rows = c_pad.shape[0]
    n = 3 * D_MODEL
    tn = 768
    return pl.pallas_call(
        _adaln_kernel,
        grid=(DEPTH, n // tn),
        in_specs=[
            pl.BlockSpec((rows, D_MODEL), lambda l, j: (0, 0)),
            pl.BlockSpec((None, D_MODEL, tn), lambda l, j: (l, 0, j)),
            pl.BlockSpec((None, 1, tn), lambda l, j: (l, 0, j)),
        ],
        out_specs=pl.BlockSpec((None, rows, tn), lambda l, j: (l, 0, j)),
        out_shape=jax.ShapeDtypeStruct((DEPTH, rows, n), F32),
        compiler_params=_params(("parallel", "parallel")),
        name="adaln",
    )(c_pad, ada_w, ada_b.reshape(DEPTH, 1, n))


def _in_proj_kernel(x_ref, scale_ref, shift_ref, w_ref, wdt_ref, o_ref, dt_ref, h_ref):
    @pl.when(pl.program_id(1) == 0)
    def _():
        h = (x_ref[...] * (1.0 + scale_ref[...]) + shift_ref[...]).astype(BF16)
        h_ref[...] = h
        dt_ref[...] = jnp.dot(h, wdt_ref[...], preferred_element_type=F32)

    o_ref[...] = jnp.dot(h_ref[...], w_ref[...], preferred_element_type=F32)


def _in_proj(x2, scale, shift, w, wdt, *, n, seq, tm, tn):
    t, d = x2.shape
    per_batch = seq // tm
    return pl.pallas_call(
        _in_proj_kernel,
        grid=(t // tm, n // tn),
        in_specs=[
            pl.BlockSpec((tm, d), lambda i, j: (i, 0)),
            pl.BlockSpec((None, 1, d), lambda i, j: (i // per_batch, 0, 0)),
            pl.BlockSpec((None, 1, d), lambda i, j: (i // per_batch, 0, 0)),
            pl.BlockSpec((d, tn), lambda i, j: (0, j)),
            pl.BlockSpec((d, LANES), lambda i, j: (0, 0)),
        ],
        out_specs=[pl.BlockSpec((tm, tn), lambda i, j: (i, j)), pl.BlockSpec((tm, LANES), lambda i, j: (i, 0))],
        out_shape=[jax.ShapeDtypeStruct((t, n), F32), jax.ShapeDtypeStruct((t, LANES), F32)],
        scratch_shapes=[pltpu.VMEM((tm, d), BF16)],
        compiler_params=_params(("parallel", "arbitrary")),
        name="ssd_in_proj",
    )(x2, scale, shift, w, wdt)


def _conv_silu(raw, carry, w, b):
    assert SSD_CONV_W == 4
    wh = 0.5 * w
    row = lax.broadcasted_iota(jnp.int32, carry.shape, 0)

    def delayed(cur, tail, s):
        sh = pltpu.roll(cur, s, axis=0)
        prev = pltpu.roll(tail, s, axis=0)
        head = jnp.where(row < s, prev, sh[:SUBLANES])
        return jnp.concatenate([head, sh[SUBLANES:]], axis=0)

    x1 = delayed(raw, carry, 1)
    near = raw * wh[3:4, :] + x1 * wh[2:3, :] + 0.5 * b
    far = raw * wh[1:2, :] + x1 * wh[0:1, :]
    far_tail = carry * wh[1:2, :] + pltpu.roll(carry, 1, axis=0) * wh[0:1, :]
    acc = near + delayed(far, far_tail, 2)
    return acc + acc * jnp.tanh(acc)


def _ssd_kernel(z_ref, x_ref, b_ref, c_ref, dt_ref, cw_ref, cb_ref, dtb_ref, alog_ref, dexp_ref, ng_ref,
                tril_ref, e_ref, sel_ref,
                y_ref,
                state_ref, carx_ref, carb_ref, carc_ref, v3_ref, ac3_ref, ybuf_ref):
    L = SSD_CHUNK
    gw = SSD_GROUP_W
    ns = SSD_D_STATE

    @pl.when(pl.program_id(1) == 0)
    def _():
        state_ref[...] = jnp.zeros(state_ref.shape, F32)
        carx_ref[...] = jnp.zeros(carx_ref.shape, F32)
        carb_ref[...] = jnp.zeros(carb_ref.shape, F32)
        carc_ref[...] = jnp.zeros(carc_ref.shape, F32)

    v = dt_ref[...] + dtb_ref[...]
    dt = jnp.maximum(v, 0.0) + jnp.log1p(jnp.exp(-jnp.abs(v)))
    a = dt * (-jnp.exp(alog_ref[...]))
    a3 = _split3(a)
    tril = tril_ref[...]
    acum = (jnp.dot(tril, a3[:, :LANES], preferred_element_type=F32)
            + jnp.dot(tril, a3[:, LANES:2 * LANES], preferred_element_type=F32)
            + jnp.dot(tril, a3[:, 2 * LANES:], preferred_element_type=F32))
    tail = jnp.exp(acum[L - 1:L, :] - acum)
    v3_ref[0:L, :] = _split3(dt)
    v3_ref[L:2 * L, :] = _split3(tail)
    v3_ref[2 * L:3 * L, :] = _split3(jnp.exp(acum))
    ac3_ref[...] = _split3(acum)

    li = lax.broadcasted_iota(jnp.int32, (L, L), 0)
    si = lax.broadcasted_iota(jnp.int32, (L, L), 1)
    causal = li >= si
    lane = lax.broadcasted_iota(jnp.int32, (L, LANES), 1)
    heads_per_tile = LANES // SSD_HEAD_DIM
    assert heads_per_tile == 2
    n_pairs = gw // LANES

    for g in range(SSD_N_GROUPS):
        xs = slice(g * gw, (g + 1) * gw)
        bs = slice(g * ns, (g + 1) * ns)
        wb = slice(SSD_D_INNER + g * ns, SSD_D_INNER + (g + 1) * ns)
        wc = slice(SSD_D_INNER + SSD_BC_DIM + g * ns, SSD_D_INNER + SSD_BC_DIM + (g + 1) * ns)

        ex = jnp.dot(v3_ref[...], e_ref[:, xs], preferred_element_type=F32)
        dt_e = ex[0:L]
        tail_e = ex[L:2 * L]
        eac_e = ex[2 * L:3 * L]
        acg = jnp.dot(ac3_ref[...], sel_ref[:, bs], preferred_element_type=F32) * LOG2E
        acg_t = acg.T

        b_raw = b_ref[:, bs]
        c_raw = c_ref[:, bs]
        bg = _conv_silu(b_raw, carb_ref[:, bs], cw_ref[:, wb], cb_ref[:, wb])
        cg = _conv_silu(c_raw, carc_ref[:, bs], cw_ref[:, wc], cb_ref[:, wc])
        carb_ref[:, bs] = b_raw[L - SUBLANES:, :]
        carc_ref[:, bs] = c_raw[L - SUBLANES:, :]
        cb16 = cg.astype(BF16)
        bb16 = bg.astype(BF16)
        cb = lax.dot_general(cb16, bb16, (((1,), (1,)), ((), ())), preferred_element_type=F32)
        cb = jnp.where(causal, cb, 0.0)

        state = state_ref[g]
        y_state = jnp.dot(cb16, state.astype(BF16), preferred_element_type=F32)

        ssq = jnp.zeros((L, LANES), F32)
        for pair in range(n_pairs):
            ps = slice(pair * LANES, (pair + 1) * LANES)
            cs = slice(g * gw + pair * LANES, g * gw + (pair + 1) * LANES)
            x_raw = x_ref[:, cs]
            xg = _conv_silu(x_raw, carx_ref[:, cs], cw_ref[:, cs], cb_ref[:, cs])
            carx_ref[:, cs] = x_raw[L - SUBLANES:, :]
            xdt = xg * dt_e[:, ps]
            ms = []
            for sub in range(heads_per_tile):
                k = pair * heads_per_tile + sub
                seg = acg[:, k:k + 1] - acg_t[k:k + 1, :]
                ms.append((cb * jnp.exp2(jnp.minimum(seg, 0.0))).astype(BF16))
            both = jnp.dot(jnp.concatenate(ms, axis=0), xdt.astype(BF16), preferred_element_type=F32)
            y = (jnp.where(lane < SSD_HEAD_DIM, both[:L], both[L:]) + y_state[:, ps] * eac_e[:, ps]
                 + xg * dexp_ref[:, cs])
            xtail = (xdt * tail_e[:, ps]).astype(BF16)
            upd = lax.dot_general(bb16, xtail, (((0,), (0,)), ((), ())), preferred_element_type=F32)
            state_ref[g, :, ps] = state[:, ps] * eac_e[L - 1:L, ps] + upd
            hz = 0.5 * z_ref[:, cs]
            y = y * (hz + hz * jnp.tanh(hz))
            ssq = ssq + y * y
            ybuf_ref[:, ps] = y
        r = lax.rsqrt(jnp.sum(ssq, axis=-1, keepdims=True) * (1.0 / gw) + RMS_EPS)
        y_ref[:, xs] = (ybuf_ref[...] * r * ng_ref[:, xs]).astype(y_ref.dtype)


def _ssd_constants():
    tril = np.tril(np.ones((SSD_CHUNK, SSD_CHUNK), np.float32))
    head_of_col = np.arange(SSD_D_INNER) // SSD_HEAD_DIM
    e = (np.arange(LANES)[:, None] == head_of_col[None, :]).astype(np.float32)
    col = np.arange(SSD_N_GROUPS * LANES)
    src = np.where(col % LANES < SSD_HEADS_PER_GROUP,
                   (col // LANES) * SSD_HEADS_PER_GROUP + col % LANES, -1)
    sel = (np.arange(LANES)[:, None] == src[None, :]).astype(np.float32)
    e3 = np.concatenate([e, e, e], axis=0)
    sel3 = np.concatenate([sel, sel, sel], axis=0)
    return jnp.asarray(tril, BF16), jnp.asarray(e3, BF16), jnp.asarray(sel3, BF16)


def _ssd_scan(zx, dt, conv_w, conv_b, dt_bias, a_log, d_skip, norm_g, *, batch, seq):
    t = zx.shape[0]
    L = SSD_CHUNK
    nc = seq // L
    G = SSD_N_GROUPS
    gw = SSD_GROUP_W
    ns = SSD_D_STATE
    tril, e3, sel3 = _ssd_constants()
    pad = LANES - SSD_N_HEADS
    dtb = jnp.pad(dt_bias, (0, pad)).reshape(1, LANES)
    alog = jnp.pad(a_log, (0, pad)).reshape(1, LANES)
    dexp = jnp.repeat(d_skip, SSD_HEAD_DIM).reshape(1, SSD_D_INNER)
    ng = norm_g.reshape(1, SSD_D_INNER)
    cb2 = conv_b.reshape(1, SSD_CONV_DIM)

    zw = SSD_D_INNER
    const = lambda b, c: (0, 0)
    in_specs = [
        pl.BlockSpec((L, zw), lambda b, c: (b * nc + c, 0)),
        pl.BlockSpec((L, zw), lambda b, c: (b * nc + c, 1)),
        pl.BlockSpec((L, SSD_BC_DIM), lambda b, c: (b * nc + c, 2 * zw // SSD_BC_DIM)),
        pl.BlockSpec((L, SSD_BC_DIM), lambda b, c: (b * nc + c, 2 * zw // SSD_BC_DIM + 1)),
        pl.BlockSpec((L, LANES), lambda b, c: (b * nc + c, 0)),
        pl.BlockSpec((SSD_CONV_W, SSD_CONV_DIM), const),
        pl.BlockSpec((1, SSD_CONV_DIM), const),
        pl.BlockSpec((1, LANES), const),
        pl.BlockSpec((1, LANES), const),
        pl.BlockSpec((1, zw), const),
        pl.BlockSpec((1, zw), const),
        pl.BlockSpec((L, L), const),
        pl.BlockSpec((3 * LANES, zw), const),
        pl.BlockSpec((3 * LANES, G * LANES), const),
    ]
    return pl.pallas_call(
        _ssd_kernel,
        grid=(batch, nc),
        in_specs=in_specs,
        out_specs=pl.BlockSpec((L, zw), lambda b, c: (b * nc + c, 0)),
        out_shape=jax.ShapeDtypeStruct((t, zw), BF16),
        scratch_shapes=[
            pltpu.VMEM((G, ns, gw), F32),
            pltpu.VMEM((SUBLANES, zw), F32),
            pltpu.VMEM((SUBLANES, SSD_BC_DIM), F32),
            pltpu.VMEM((SUBLANES, SSD_BC_DIM), F32),
            pltpu.VMEM((3 * L, 3 * LANES), BF16),
            pltpu.VMEM((L, 3 * LANES), BF16),
            pltpu.VMEM((L, gw), F32),
        ],
        compiler_params=_params(("parallel", "arbitrary")),
        name="ssd_scan",
    )(zx, zx, zx, zx, dt, conv_w, cb2, dtb, alog, dexp, ng, tril, e3, sel3)


def _proj_ln_kernel(y_ref, w_ref, x_ref, gate_ref, g_ref, b_ref, o_ref):
    acc = jnp.dot(y_ref[...], w_ref[...], preferred_element_type=F32)
    u = DEEPNORM_ALPHA * x_ref[...] + (1.0 + gate_ref[...]) * acc
    o_ref[...] = _layer_norm(u, g_ref[...], b_ref[...])


def _proj_ln(y, w, x2, gate, ln_g, ln_b, *, seq, tm):
    t, kdim = y.shape
    d = w.shape[1]
    per_batch = seq // tm
    return pl.pallas_call(
        _proj_ln_kernel,
        grid=(t // tm,),
        in_specs=[
            pl.BlockSpec((tm, kdim), lambda i: (i, 0)),
            pl.BlockSpec((kdim, d), lambda i: (0, 0), pipeline_mode=pl.Buffered(1)),
            pl.BlockSpec((tm, d), lambda i: (i, 0)),
            pl.BlockSpec((None, 1, d), lambda i: (i // per_batch, 0, 0)),
            pl.BlockSpec((1, d), lambda i: (0, 0)),
            pl.BlockSpec((1, d), lambda i: (0, 0)),
        ],
        out_specs=pl.BlockSpec((tm, d), lambda i: (i, 0)),
        out_shape=jax.ShapeDtypeStruct((t, d), F32),
        compiler_params=_params(("parallel",)),
        name="out_proj_ln",
    )(y, w, x2, gate, ln_g.reshape(1, d), ln_b.reshape(1, d))


def _b_proj_kernel(x_ref, scale_ref, shift_ref, w_ref, o0_ref, o1_ref, o2_ref, h_ref, res_ref, tmp_ref):
    tm = x_ref.shape[0]
    n_qkv = 3 * DIL_N_GROUPS
    ow = DIL_OUT_WIDTH
    block_tiles = ow // LANES
    tiles = 2
    n_chunks = w_ref.shape[1] // (tiles * LANES)
    step_blocks = w_ref.shape[1] // ow
    out_refs = (o0_ref, o1_ref, o2_ref)

    @pl.when(pl.program_id(1) == 0)
    def _():
        x = x_ref[...]
        h_ref[0] = (x * (1.0 + scale_ref[...]) + shift_ref[...]).astype(BF16)
        h_ref[1] = x.astype(BF16)

    def emit_tile(block, gt):
        cols = slice((gt % block_tiles) * LANES, (gt % block_tiles + 1) * LANES)
        out_ref = out_refs[block % DIL_N_GROUPS]
        dilation = DIL_PATTERNS[block % DIL_N_GROUPS][1]
        rows = tm // dilation
        if dilation <= SHUFFLE_STRIDE:
            for r in range(dilation):
                out_ref[r, :, cols] = res_ref[gt, pl.ds(r, rows, stride=dilation), :].astype(BF16)
        else:
            assert dilation == SHUFFLE_STRIDE * SHUFFLE_STRIDE
            tmp = tmp_ref.at[gt % 2]
            for r_lo in range(SHUFFLE_STRIDE):
                tmp[r_lo] = res_ref[gt, pl.ds(r_lo, tm // SHUFFLE_STRIDE, stride=SHUFFLE_STRIDE), :]
            for r_lo in range(SHUFFLE_STRIDE):
                for r_hi in range(SHUFFLE_STRIDE):
                    out_ref[r_lo + SHUFFLE_STRIDE * r_hi, :, cols] = (
                        tmp[r_lo, pl.ds(r_hi, rows, stride=SHUFFLE_STRIDE), :].astype(BF16))

    def step(jj):
        def block_of(gt):
            return jj * step_blocks + gt // block_tiles

        def matmul_chunk(c):
            block = block_of(c * tiles)
            plain = 1 if DIL_N_GROUPS <= block < n_qkv else 0
            cols = slice(c * tiles * LANES, (c + 1) * tiles * LANES)
            acc = jnp.dot(h_ref[plain], w_ref[:, cols], preferred_element_type=F32)
            for t in range(tiles):
                res_ref[c * tiles + t] = acc[:, t * LANES:(t + 1) * LANES]

        matmul_chunk(0)
        for c in range(1, n_chunks + 1):
            if c < n_chunks:
                matmul_chunk(c)
            for gt in range((c - 1) * tiles, c * tiles):
                emit_tile(block_of(gt), gt)

    for jj in range(n_qkv // step_blocks):
        pl.when(pl.program_id(1) == jj)(lambda jj=jj: step(jj))


def _b_proj(x2, scale, shift, w, *, batch, seq, tm):
    t, d = x2.shape
    ow = DIL_OUT_WIDTH
    per_batch = seq // tm
    step_blocks = DIL_N_GROUPS
    assert w.shape[1] == 3 * DIL_N_GROUPS * ow
    out_specs, out_shape = [], []
    for _, dilation in DIL_PATTERNS:
        out_specs.append(pl.BlockSpec(
            (None, dilation, tm // dilation, ow), lambda i, j: (i // per_batch, 0, i % per_batch, j)))
        out_shape.append(jax.ShapeDtypeStruct((batch, dilation, seq // dilation, 3 * ow), BF16))
    return pl.pallas_call(
        _b_proj_kernel,
        grid=(t // tm, w.shape[1] // (step_blocks * ow)),
        in_specs=[
            pl.BlockSpec((tm, d), lambda i, j: (i, 0)),
            pl.BlockSpec((None, 1, d), lambda i, j: (i // per_batch, 0, 0)),
            pl.BlockSpec((None, 1, d), lambda i, j: (i // per_batch, 0, 0)),
            pl.BlockSpec((d, step_blocks * ow), lambda i, j: (0, j)),
        ],
        out_specs=out_specs,
        out_shape=out_shape,
        scratch_shapes=[pltpu.VMEM((2, tm, d), BF16), pltpu.VMEM((step_blocks * ow // LANES, tm, LANES), F32),
                        pltpu.VMEM((2, SHUFFLE_STRIDE, tm // SHUFFLE_STRIDE, LANES), F32)],
        compiler_params=_params(("parallel", "arbitrary")),
        name="dilated_in_proj",
    )(x2, scale, shift, w)


def _gate_proj_kernel(x_ref, scale_ref, shift_ref, w_ref, o_ref):
    h = (x_ref[...] * (1.0 + scale_ref[...]) + shift_ref[...]).astype(BF16)
    o_ref[...] = jnp.dot(h, w_ref[...], preferred_element_type=F32)


def _gate_proj(x2, scale, shift, w, *, seq, tm):
    t, d = x2.shape
    n = w.shape[1]
    per_batch = seq // tm
    return pl.pallas_call(
        _gate_proj_kernel,
        grid=(t // tm,),
        in_specs=[
            pl.BlockSpec((tm, d), lambda i: (i, 0)),
            pl.BlockSpec((None, 1, d), lambda i: (i // per_batch, 0, 0)),
            pl.BlockSpec((None, 1, d), lambda i: (i // per_batch, 0, 0)),
            pl.BlockSpec((d, n), lambda i: (0, 0)),
        ],
        out_specs=pl.BlockSpec((tm, n), lambda i: (i, 0)),
        out_shape=jax.ShapeDtypeStruct((t, n), F32),
        compiler_params=_params(("parallel",)),
        name="dilated_gate_proj",
    )(x2, scale, shift, w)


def _attn_bias(group):
    _, dilation = DIL_PATTERNS[group]
    n_all = DIL_N_GROUPS * DIL_HEADS
    slopes = 2.0 ** (-8.0 * np.arange(1, n_all + 1) / n_all)
    slopes = slopes.reshape(DIL_N_GROUPS, DIL_HEADS)[group].astype(np.float32)
    qi = np.arange(DIL_BLOCK)[:, None]
    kj = np.arange(2 * DIL_BLOCK)[None, :]
    delta = qi + DIL_BLOCK - kj
    valid = (delta >= 0) & (delta <= DIL_BLOCK)
    alibi = -slopes[:, None, None] * (delta * dilation).astype(np.float32)[None]
    return jnp.asarray(np.where(valid[None], alibi * np.float32(LOG2E), -np.inf).astype(np.float32))


def _attn_kernel(q_ref, k_ref, v_ref, bias_ref, o_ref, st_ref, kp_ref, vp_ref):
    first = pl.program_id(2) == 0
    blk = DIL_BLOCK
    n_sub, tq = q_ref.shape[0], q_ref.shape[1]
    scale = DIL_HEAD_DIM ** -0.5 * LOG2E
    dn = (((1,), (1,)), ((), ()))
    lane = lax.broadcasted_iota(jnp.int32, (blk, LANES), 1)
    kcol = lax.broadcasted_iota(jnp.int32, (blk, 2 * blk), 1)
    no_prev = first & (kcol < blk)

    @pl.when(first)
    def _():
        kp_ref[...] = jnp.zeros(kp_ref.shape, kp_ref.dtype)
        vp_ref[...] = jnp.zeros(vp_ref.shape, vp_ref.dtype)

    for r in range(n_sub):
        for sb in range(tq // blk):
            rows = slice(sb * blk, (sb + 1) * blk)
            keys = slice((sb - 1) * blk, (sb + 1) * blk)
            m_tile = jnp.zeros((blk, LANES), F32)
            den_tile = jnp.ones((blk, LANES), F32)
            ss = []
            for h in range(DIL_HEADS):
                cols = slice(h * DIL_HEAD_DIM, (h + 1) * DIL_HEAD_DIM)
                q = q_ref[r, rows, cols]
                if sb == 0:
                    s = jnp.concatenate(
                        [lax.dot_general(q, kp_ref[r, :, cols], dn, preferred_element_type=F32),
                         lax.dot_general(q, k_ref[r, rows, cols], dn, preferred_element_type=F32)], axis=1)
                else:
                    s = lax.dot_general(q, k_ref[r, keys, cols], dn, preferred_element_type=F32)
                ss.append(s)
            ps = []
            for h in range(DIL_HEADS):
                bias = bias_ref[h]
                if sb == 0:
                    bias = jnp.where(no_prev, -jnp.inf, bias)
                s = ss[h] * scale + bias
                m = jnp.max(s, axis=-1, keepdims=True)
                p = jnp.exp2(s - m)
                den = jnp.sum(p, axis=-1, keepdims=True)
                ps.append(p.astype(BF16))
                m_tile = jnp.where(lane == h, m, m_tile)
                den_tile = jnp.where((lane == h) | (lane == DIL_HEADS + h), den, den_tile)
            for h in range(DIL_HEADS):
                cols = slice(h * DIL_HEAD_DIM, (h + 1) * DIL_HEAD_DIM)
                p16 = ps[h]
                if sb == 0:
                    o = (jnp.dot(p16[:, :blk], vp_ref[r, :, cols], preferred_element_type=F32)
                         + jnp.dot(p16[:, blk:], v_ref[r, rows, cols], preferred_element_type=F32))
                else:
                    o = jnp.dot(p16, v_ref[r, keys, cols], preferred_element_type=F32)
                o_ref[r, rows, cols] = o
            st_ref[r, rows, :] = jnp.where(lane < DIL_HEADS, (m_tile + jnp.log2(den_tile)) * (1.0 / LOG2E),
                                           1.0 / den_tile)
    for r in range(n_sub):
        kp_ref[r] = k_ref[r, tq - blk:, :]
        vp_ref[r] = v_ref[r, tq - blk:, :]


def _dilated_attention(qkv, group, *, n_sub, tq):
    batch, dilation, m, _ = qkv.shape
    assert DIL_PATTERNS[group][0] // dilation == DIL_BLOCK and m % tq == 0 and dilation % n_sub == 0
    w = DIL_OUT_WIDTH
    blk = DIL_BLOCK
    return pl.pallas_call(
        _attn_kernel,
        grid=(batch, dilation // n_sub, m // tq),
        in_specs=[
            pl.BlockSpec((None, n_sub, tq, w), lambda b, r, i: (b, r, i, 0)),
            pl.BlockSpec((None, n_sub, tq, w), lambda b, r, i: (b, r, i, 1)),
            pl.BlockSpec((None, n_sub, tq, w), lambda b, r, i: (b, r, i, 2)),
            pl.BlockSpec((DIL_HEADS, blk, 2 * blk), lambda b, r, i: (0, 0, 0)),
        ],
        out_specs=[
            pl.BlockSpec((None, n_sub, tq, w), lambda b, r, i: (b, r, i, 0)),
            pl.BlockSpec((None, n_sub, tq, LANES), lambda b, r, i: (b, r, i, 0)),
        ],
        out_shape=[
            jax.ShapeDtypeStruct((batch, dilation, m, w), F32),
            jax.ShapeDtypeStruct((batch, dilation, m, LANES), F32),
        ],
        scratch_shapes=[pltpu.VMEM((n_sub, blk, w), BF16), pltpu.VMEM((n_sub, blk, w), BF16)],
        compiler_params=_params(("parallel", "parallel", "arbitrary")),
        name=f"dilated_attn_{group}",
    )(qkv, qkv, qkv, _attn_bias(group))


def _merge_ln_kernel(o0_ref, o1_ref, o2_ref, s0_ref, s1_ref, s2_ref, z_ref, w_ref, x_ref,
                     gate_ref, g_ref, b_ref, out_ref, on_ref, sn_ref, tmp_ref):
    tm = x_ref.shape[0]
    assert DIL_PATTERNS[0][1] == 1
    def scatter(dst, src_ref, cols, dilation, tmp):
        rows = tm // dilation
        if dilation <= SHUFFLE_STRIDE:
            for r in range(dilation):
                dst[pl.ds(r, rows, stride=dilation), :] = src_ref[r, :, cols]
        else:
            assert dilation == SHUFFLE_STRIDE * SHUFFLE_STRIDE
            for r_lo in range(SHUFFLE_STRIDE):
                for r_hi in range(SHUFFLE_STRIDE):
                    tmp[r_lo, pl.ds(r_hi, rows, stride=SHUFFLE_STRIDE), :] = (
                        src_ref[r_lo + SHUFFLE_STRIDE * r_hi, :, cols])
            for r_lo in range(SHUFFLE_STRIDE):
                dst[pl.ds(r_lo, tm // SHUFFLE_STRIDE, stride=SHUFFLE_STRIDE), :] = tmp[r_lo]

    n_tmp = tmp_ref.shape[0]
    for g, (o_ref, s_ref) in ((1, (o1_ref, s1_ref)), (2, (o2_ref, s2_ref))):
        dilation = DIL_PATTERNS[g][1]
        scatter(sn_ref.at[g - 1], s_ref, slice(0, LANES), dilation, tmp_ref.at[0])
        for h in range(DIL_HEADS):
            scatter(on_ref.at[g - 1, h], o_ref, slice(h * DIL_HEAD_DIM, (h + 1) * DIL_HEAD_DIM), dilation,
                    tmp_ref.at[h % n_tmp])
    l0, l1, l2 = s0_ref[0], sn_ref[0], sn_ref[1]
    mx = jnp.maximum(jnp.maximum(l0, l1), l2)
    e0, e1, e2 = jnp.exp(l0 - mx), jnp.exp(l1 - mx), jnp.exp(l2 - mx)
    inv = 1.0 / (e0 + e1 + e2)
    to_head_lane = LANES - DIL_HEADS
    c0 = e0 * inv * pltpu.roll(l0, to_head_lane, axis=1)
    c1 = e1 * inv * pltpu.roll(l1, to_head_lane, axis=1)
    c2 = e2 * inv * pltpu.roll(l2, to_head_lane, axis=1)
    pieces = []
    for h in range(DIL_HEADS):
        pieces.append(o0_ref[0, :, h * DIL_HEAD_DIM:(h + 1) * DIL_HEAD_DIM] * c0[:, h:h + 1]
                      + on_ref[0, h] * c1[:, h:h + 1]
                      + on_ref[1, h] * c2[:, h:h + 1])
    hz = 0.5 * z_ref[...]
    o = jnp.concatenate(pieces, axis=1) * (hz + hz * jnp.tanh(hz))
    y = jnp.dot(o.astype(BF16), w_ref[...], preferred_element_type=F32)
    u = DEEPNORM_ALPHA * x_ref[...] + (1.0 + gate_ref[...]) * y
    out_ref[...] = _layer_norm(u, g_ref[...], b_ref[...])


def _merge_ln(os_, stats, z, w, x2, gate, ln_g, ln_b, *, seq, tm):
    t, d = x2.shape
    ow = DIL_OUT_WIDTH
    per_batch = seq // tm
    row = lambda i: (i, 0)

    def sub_major(width):
        return [pl.BlockSpec((None, dilation, tm // dilation, width),
                             lambda i: (i // per_batch, 0, i % per_batch, 0))
                for _, dilation in DIL_PATTERNS]

    return pl.pallas_call(
        _merge_ln_kernel,
        grid=(t // tm,),
        in_specs=sub_major(ow) + sub_major(LANES) + [
            pl.BlockSpec((tm, ow), row),
            pl.BlockSpec((ow, d), lambda i: (0, 0)),
            pl.BlockSpec((tm, d), row),
            pl.BlockSpec((None, 1, d), lambda i: (i // per_batch, 0, 0)),
            pl.BlockSpec((1, d), lambda i: (0, 0)),
            pl.BlockSpec((1, d), lambda i: (0, 0)),
        ],
        out_specs=pl.BlockSpec((tm, d), row),
        out_shape=jax.ShapeDtypeStruct((t, d), F32),
        scratch_shapes=[pltpu.VMEM((DIL_N_GROUPS - 1, DIL_HEADS, tm, LANES), F32),
                        pltpu.VMEM((DIL_N_GROUPS - 1, tm, LANES), F32),
                        pltpu.VMEM((2, SHUFFLE_STRIDE, tm // SHUFFLE_STRIDE, LANES), F32)],
        compiler_params=_params(("parallel",)),
        name="merge_out_proj_ln",
    )(*os_, *stats, z, w, x2, gate, ln_g.reshape(1, d), ln_b.reshape(1, d))


def _ssd_block(x2, scale, shift, gate, in_w, conv_w, conv_b, dt_bias, a_log, d_skip, norm_g, out_w, ln_g, ln_b,
               *, batch, seq):
    w16 = in_w.astype(BF16)
    dt_w = jnp.pad(w16[:, SSD_D_INNER + SSD_CONV_DIM:], ((0, 0), (0, LANES - SSD_N_HEADS)))
    zx, dt = _in_proj(x2, scale, shift, w16, dt_w, n=SSD_D_INNER + SSD_CONV_DIM, seq=seq, tm=1024, tn=1280)
    y = _ssd_scan(zx, dt, conv_w, conv_b, dt_bias, a_log, d_skip, norm_g, batch=batch, seq=seq)
    return _proj_ln(y, out_w.astype(BF16), x2, gate, ln_g, ln_b, seq=seq, tm=512)


def _dilated_block(x2, scale, shift, gate, kv_w, in_w, out_w, ln_g, ln_b, *, batch, seq):
    w = jnp.concatenate([in_w[:, :DIL_Q_WIDTH], kv_w], axis=1).astype(BF16)
    qkvs = _b_proj(x2, scale, shift, w, batch=batch, seq=seq, tm=512)
    z = _gate_proj(x2, scale, shift, in_w[:, DIL_Q_WIDTH:].astype(BF16), seq=seq, tm=1024)
    os_, stats = [], []
    for group in range(DIL_N_GROUPS):
        tq = min(ATTN_ROWS, seq // DIL_PATTERNS[group][1])
        o, st = _dilated_attention(qkvs[group], group, n_sub=ATTN_ROWS // tq, tq=tq)
        os_.append(o)
        stats.append(st)
    return _merge_ln(os_, stats, z, out_w.astype(BF16), x2, gate, ln_g, ln_b, seq=seq, tm=512)


def kernel(x, c, ada_w, ada_b, ln_g, ln_b, a_in_w, a_conv_w, a_conv_b, a_dt_bias, a_A_log, a_D,
           a_norm_g, a_out_w, kv_w, b_in_w, b_out_w):
    batch, seq, d = x.shape
    x2 = x.reshape(batch * seq, d)

    c_pad = jnp.pad(c, ((0, 2 * SUBLANES - batch), (0, 0)))
    mod = _adaln(c_pad, ada_w, ada_b)[:, :batch]
    shift = mod[:, :, None, 0:d]
    scale = mod[:, :, None, d:2 * d]
    gate = mod[:, :, None, 2 * d:3 * d]

    x2 = _ssd_block(x2, scale[0], shift[0], gate[0], a_in_w[0], a_conv_w[0], a_conv_b[0], a_dt_bias[0],
                    a_A_log[0], a_D[0], a_norm_g[0], a_out_w[0], ln_g[0], ln_b[0], batch=batch, seq=seq)
    x2 = _dilated_block(x2, scale[1], shift[1], gate[1], kv_w, b_in_w[0], b_out_w[0], ln_g[1], ln_b[1],
                        batch=batch, seq=seq)
    return x2.reshape(batch, seq, d)
```
